```python
import math
import jax
import jax.numpy as jnp
from jax import lax
import numpy as np


D_MODEL = 1024
BATCH = 4
SEQ = 4096
DEPTH = 4

GRID_W = 64
CTX_LEN = 256
HEAD_DIM = 64
NA_HEADS = 16
NA_WIDTH = NA_HEADS * HEAD_DIM
WIN_R = 8
WIN_C = 16
QB_C = 16
KB_C = QB_C + WIN_C
ROPE_BASE = 10000.0
ATTN_SCALE = HEAD_DIM ** -0.5
SSD_D_INNER = 2 * D_MODEL
SSD_HEADDIM = 64
SSD_HEADS = SSD_D_INNER // SSD_HEADDIM
SSD_GROUPS = 4
SSD_STATE = 128
SSD_CONV = 5
SSD_CHUNK = 128
SSD_CONV_DIM = SSD_D_INNER + 2 * SSD_GROUPS * SSD_STATE
D_FF = 2816
N_BRANCH = 2
N_MOD = 9
EPS = 1e-6
IN_SIZES = (NA_WIDTH, NA_WIDTH, NA_WIDTH, SSD_D_INNER, SSD_CONV_DIM, 2 * SSD_HEADS, N_BRANCH * D_MODEL)
N_IN = 3 * NA_WIDTH + SSD_D_INNER + SSD_CONV_DIM + 2 * SSD_HEADS + N_BRANCH * D_MODEL

kernel_name = 'hybrid_na_ssd_macaron_dit'


def rmsnorm(x, g):
    xf = x.astype(jnp.float32)
    y = xf * lax.rsqrt(jnp.mean(xf * xf, axis=-1, keepdims=True) + EPS)
    return (y * g.astype(jnp.float32)).astype(x.dtype)


def modulate(h, shift, scale):
    return h * (1.0 + scale) + shift


def adaln(v, w, b):
    m = jax.nn.silu(v) @ w + b
    return m.reshape(m.shape[:-1] + (N_MOD, D_MODEL))


def swiglu(h, w_gate, w_up, w_down):
    return (jax.nn.silu(h @ w_gate) * (h @ w_up)) @ w_down


def ffn_sublayer(s, mod, i, g_norm, w_gate, w_up, w_down):
    h = modulate(rmsnorm(s, g_norm), mod[:, :, i], mod[:, :, i + 1])
    return s + 0.5 * mod[:, :, i + 2] * swiglu(h, w_gate, w_up, w_down)


def split_in(p):
    offs = np.cumsum(IN_SIZES)[:-1].tolist()
    return jnp.split(p, offs, axis=-1)


def rope_2d(n_tokens):
    t = jnp.arange(n_tokens, dtype=jnp.int32)
    row = (t // GRID_W).astype(jnp.float32)
    col = (t % GRID_W).astype(jnp.float32)
    n_freq = HEAD_DIM // 4
    inv = ROPE_BASE ** (-jnp.arange(n_freq, dtype=jnp.float32) / n_freq)
    ang = jnp.concatenate([row[:, None] * inv, col[:, None] * inv], axis=-1)[:, None, :]
    return jnp.cos(ang), jnp.sin(ang)


def apply_rope(x, cos, sin):
    half = x.shape[-1] // 2
    xf = x.astype(jnp.float32)
    x1, x2 = xf[..., :half], xf[..., half:]
    return jnp.concatenate([x1 * cos - x2 * sin, x1 * sin + x2 * cos], axis=-1).astype(x.dtype)


def heads(t, n_heads, dim):
    return t.reshape(t.shape[0], t.shape[1], n_heads, dim)


def neighbourhood_attention(q, k, v, k_ctx, v_ctx, rpb):
    bsz, n_lat, n_h, hd = q.shape
    rows = n_lat // GRID_W
    win_r = min(WIN_R, rows)
    n_cb = GRID_W // QB_C
    r_idx = np.arange(rows)
    r0 = np.clip(r_idx - win_r // 2, 0, rows - win_r)
    dr = r0[:, None] + np.arange(win_r)[None, :] - r_idx[:, None]
    key_cols = np.clip(np.arange(n_cb) * QB_C - WIN_C // 2, 0, GRID_W - KB_C)[:, None] + np.arange(KB_C)[None, :]
    q_cols = np.arange(GRID_W).reshape(n_cb, QB_C)
    s_col = np.clip(q_cols - WIN_C // 2, 0, GRID_W - WIN_C)
    valid = (key_cols[:, None, :] >= s_col[..., None]) & (key_cols[:, None, :] < s_col[..., None] + WIN_C)
    dc_idx = np.clip(key_cols[:, None, :] - q_cols[..., None], -(WIN_C - 1), WIN_C - 1) + (WIN_C - 1)
    n_win = win_r * KB_C

    kg = k.reshape(bsz, rows, GRID_W, n_h, hd)
    vg = v.reshape(bsz, rows, GRID_W, n_h, hd)
    q_rows = jnp.moveaxis(q.reshape(bsz, rows, n_cb, QB_C, n_h, hd), 1, 0)

    def row_block(args):
        q_r, r0_r, dr_r = args
        k_rows = lax.dynamic_slice_in_dim(kg, r0_r, win_r, axis=1)
        v_rows = lax.dynamic_slice_in_dim(vg, r0_r, win_r, axis=1)
        k_blk = k_rows[:, :, key_cols]
        v_blk = v_rows[:, :, key_cols]
        s_win = jnp.einsum('bnqhd,binkhd->bhnqik', q_r, k_blk).astype(jnp.float32) * ATTN_SCALE
        bias = rpb[:, dr_r + (WIN_R - 1)][:, :, dc_idx]
        s_win = s_win + jnp.transpose(bias, (0, 2, 3, 1, 4)).astype(jnp.float32)[None]
        s_win = jnp.where(valid[:, :, None, :], s_win, -jnp.inf)
        s_ctx = jnp.einsum('bnqhd,bchd->bhnqc', q_r, k_ctx).astype(jnp.float32) * ATTN_SCALE
        s_all = jnp.concatenate([s_win.reshape(bsz, n_h, n_cb, QB_C, n_win), s_ctx], axis=-1)
        p = jax.nn.softmax(s_all, axis=-1).astype(v.dtype)
        p_win = p[..., :n_win].reshape(bsz, n_h, n_cb, QB_C, win_r, KB_C)
        o = jnp.einsum('bhnqik,binkhd->bnqhd', p_win, v_blk) + jnp.einsum('bhnqc,bchd->bnqhd', p[..., n_win:], v_ctx)
        return o.reshape(bsz, GRID_W, n_h, hd)

    out = lax.map(row_block, (q_rows, jnp.asarray(r0, jnp.int32), jnp.asarray(dr, jnp.int32)))
    return jnp.moveaxis(out, 0, 1).reshape(bsz, n_lat, n_h, hd)


def context_attention(q, k, v):
    s = jnp.einsum('bqhd,bkhd->bhqk', q, k).astype(jnp.float32) * ATTN_SCALE
    p = jax.nn.softmax(s, axis=-1).astype(v.dtype)
    return jnp.einsum('bhqk,bkhd->bqhd', p, v)


def attention_branch(q_l, k_l, v_l, q_c, k_c, v_c, q_norm, k_norm, rpb, cos, sin, need_ctx):
    bsz, n_lat, _ = q_l.shape
    hh = lambda t: heads(t, NA_HEADS, HEAD_DIM)
    ql = apply_rope(rmsnorm(hh(q_l), q_norm), cos, sin)
    kl = apply_rope(rmsnorm(hh(k_l), k_norm), cos, sin)
    kc = rmsnorm(hh(k_c), k_norm)
    vc = hh(v_c)
    o_l = neighbourhood_attention(ql, kl, hh(v_l), kc, vc, rpb).reshape(bsz, n_lat, NA_WIDTH)
    o_c = None
    if need_ctx:
        o_c = context_attention(rmsnorm(hh(q_c), q_norm), kc, vc).reshape(bsz, q_c.shape[1], NA_WIDTH)
    return o_l, o_c


def dwconv_centred(x, w, b):
    k_w, ch = w.shape
    out = lax.conv_general_dilated(x, w[:, None, :].astype(x.dtype), window_strides=(1,),
                                   padding=[(k_w // 2, k_w // 2)],
                                   dimension_numbers=('NWC', 'WIO', 'NWC'), feature_group_count=ch)
    return out + b


def ssd_chunked(x, dt, a, bm, cm, state0, want_y):
    bsz, n_tok, n_h, p_dim = x.shape
    n_g, n_s = bm.shape[2], bm.shape[3]
    hg = n_h // n_g
    q_len = math.gcd(n_tok, SSD_CHUNK)
    nc = n_tok // q_len
    xf = x.astype(jnp.float32).reshape(bsz, nc, q_len, n_g, hg, p_dim)
    dtc = dt.astype(jnp.float32).reshape(bsz, nc, q_len, n_g, hg)
    bc = bm.astype(jnp.float32).reshape(bsz, nc, q_len, n_g, n_s)
    cc = cm.astype(jnp.float32).reshape(bsz, nc, q_len, n_g, n_s)
    a_cum = jnp.cumsum(dtc * a.reshape(n_g, hg), axis=2)
    a_last = a_cum[:, :, -1]
    xdt = xf * dtc[..., None]
    states = jnp.einsum('bcjgn,bcjghp->bcghpn', bc, xdt * jnp.exp(a_last[:, :, None] - a_cum)[..., None])

    def step(s, inp):
        decay, st = inp
        return s * decay[..., None, None] + st, s

    s_final, s_enter = lax.scan(step, state0.astype(jnp.float32),
                                (jnp.moveaxis(jnp.exp(a_last), 1, 0), jnp.moveaxis(states, 1, 0)))
    if not want_y:
        return None, s_final
    s_enter = jnp.moveaxis(s_enter, 0, 1)
    at = jnp.moveaxis(a_cum, 2, -1)
    tri = np.tril(np.ones((q_len, q_len), dtype=bool))
    l_mat = jnp.exp(jnp.where(tri, at[..., :, None] - at[..., None, :], -jnp.inf))
    cb = jnp.einsum('bcign,bcjgn->bcgij', cc, bc)
    y_diag = jnp.einsum('bcghij,bcjghp->bcighp', cb[:, :, :, None] * l_mat, xdt)
    y_off = jnp.einsum('bcign,bcghpn->bcighp', cc, s_enter) * jnp.exp(a_cum)[..., None]
    y = (y_diag + y_off).reshape(bsz, n_tok, n_h, p_dim).astype(x.dtype)
    return y, s_final


def ssd_branch(z_l, xbc_l, dt_l, z_c, xbc_c, dt_c, conv_w, conv_b, dt_bias, a_log, d_skip, norm_w, need_ctx):
    a = -jnp.exp(a_log.astype(jnp.float32))

    def prep(xbc, dt_raw):
        bsz, n_tok, _ = xbc.shape
        u = jax.nn.silu(dwconv_centred(xbc, conv_w, conv_b))
        xs, bm, cm = jnp.split(u, [SSD_D_INNER, SSD_D_INNER + SSD_GROUPS * SSD_STATE], axis=-1)
        dt = jax.nn.softplus(dt_raw.astype(jnp.float32).reshape(bsz, n_tok, 2, SSD_HEADS) + dt_bias.astype(jnp.float32))
        return (heads(xs, SSD_HEADS, SSD_HEADDIM), heads(bm, SSD_GROUPS, SSD_STATE),
                heads(cm, SSD_GROUPS, SSD_STATE), dt)

    xl, bl, cl, dtl = prep(xbc_l, dt_l)
    xc, bcx, ccx, dtc = prep(xbc_c, dt_c)
    bsz = xl.shape[0]
    s0 = jnp.zeros((bsz, SSD_GROUPS, SSD_HEADS // SSD_GROUPS, SSD_HEADDIM, SSD_STATE), jnp.float32)
    flip = lambda t: jnp.flip(t, axis=1)
    yc_f, sc_f = ssd_chunked(xc, dtc[:, :, 0], a[0], bcx, ccx, s0, need_ctx)
    yl_f, _ = ssd_chunked(xl, dtl[:, :, 0], a[0], bl, cl, sc_f, True)
    yc_b, sc_b = ssd_chunked(flip(xc), flip(dtc[:, :, 1]), a[1], flip(bcx), flip(ccx), s0, need_ctx)
    yl_b, _ = ssd_chunked(flip(xl), flip(dtl[:, :, 1]), a[1], flip(bl), flip(cl), sc_b, True)

    def finish(y_f, y_b_rev, xs, z):
        b_, n_tok = xs.shape[0], xs.shape[1]
        y = y_f + flip(y_b_rev) + d_skip[:, None] * xs
        y = (y.reshape(b_, n_tok, SSD_D_INNER) * jax.nn.silu(z)).astype(jnp.float32)
        yg = y.reshape(b_, n_tok, SSD_GROUPS, SSD_D_INNER // SSD_GROUPS)
        yg = yg * lax.rsqrt(jnp.mean(yg * yg, axis=-1, keepdims=True) + EPS)
        return (yg.reshape(b_, n_tok, SSD_D_INNER) * norm_w.astype(jnp.float32)).astype(xs.dtype)

    y_l = finish(yl_f, yl_b, xl, z_l)
    y_c = finish(yc_f, yc_b, xc, z_c) if need_ctx else None
    return y_l, y_c


def hybrid_mixer(h_l, h_c, w_in, q_norm, k_norm, na_rpb, na_w_o, ssd_conv_w, ssd_conv_b, ssd_dt_bias,
                 ssd_a_log, ssd_d, ssd_norm, ssd_w_o, w_out, cos, sin, need_ctx):
    q_l, k_l, v_l, z_l, xbc_l, dt_l, g_l = split_in(h_l @ w_in)
    q_c, k_c, v_c, z_c, xbc_c, dt_c, g_c = split_in(h_c @ w_in)
    a_l, a_c = attention_branch(q_l, k_l, v_l, q_c, k_c, v_c, q_norm, k_norm, na_rpb, cos, sin, need_ctx)
    s_l, s_c = ssd_branch(z_l, xbc_l, dt_l, z_c, xbc_c, dt_c, ssd_conv_w, ssd_conv_b, ssd_dt_bias,
                          ssd_a_log, ssd_d, ssd_norm, need_ctx)

    def merge(a_o, s_o, g):
        g = jax.nn.sigmoid(g.astype(jnp.float32)).astype(a_o.dtype)
        g_a, g_s = jnp.split(g, N_BRANCH, axis=-1)
        return (g_a * (a_o @ na_w_o) + g_s * (s_o @ ssd_w_o)) @ w_out

    y_l = merge(a_l, s_l, g_l)
    y_c = merge(a_c, s_c, g_c) if need_ctx else None
    return y_l, y_c


def setup_inputs(seed: int = 0) -> dict:
    key = jax.random.key(seed)
    ks = iter(jax.random.split(key, 40))
    f32 = jnp.float32

    def nrm(shape, scale):
        return jax.random.normal(next(ks), shape, f32) * scale

    def gain(shape):
        return 1.0 + nrm(shape, 0.05)

    L = DEPTH
    dt0 = jnp.exp(jax.random.uniform(next(ks), (L, 2, SSD_HEADS), f32, math.log(1e-3), math.log(1e-1)))
    dt_bias = dt0 + jnp.log(-jnp.expm1(-dt0))
    a_log = jnp.log(jax.random.uniform(next(ks), (L, 2, SSD_HEADS), f32, 1.0, 16.0))
    return {
        'x': nrm((BATCH, SEQ, D_MODEL), 1.0),
        'c': nrm((BATCH, D_MODEL), 1.0),
        'ctx': nrm((BATCH, CTX_LEN, D_MODEL), 1.0),
        'c_ctx': nrm((D_MODEL,), 1.0),
        'w_ada': nrm((L, D_MODEL, N_MOD * D_MODEL), 0.5 * D_MODEL ** -0.5),
        'b_ada': nrm((L, N_MOD * D_MODEL), 0.05),
        'norm_ffn1': gain((L, D_MODEL)),
        'ffn1_w_gate': nrm((L, D_MODEL, D_FF), D_MODEL ** -0.5),
        'ffn1_w_up': nrm((L, D_MODEL, D_FF), D_MODEL ** -0.5),
        'ffn1_w_down': nrm((L, D_FF, D_MODEL), D_FF ** -0.5),
        'norm_mix': gain((L, D_MODEL)),
        'w_in': nrm((L, D_MODEL, N_IN), D_MODEL ** -0.5),
        'q_norm': gain((L, HEAD_DIM)),
        'k_norm': gain((L, HEAD_DIM)),
        'na_rpb': nrm((L, NA_HEADS, 2 * WIN_R - 1, 2 * WIN_C - 1), 0.1),
        'na_w_o': nrm((L, NA_WIDTH, D_MODEL), NA_WIDTH ** -0.5),
        'ssd_conv_w': nrm((L, SSD_CONV, SSD_CONV_DIM), SSD_CONV ** -0.5),
        'ssd_conv_b': nrm((L, SSD_CONV_DIM), 0.02),
        'ssd_dt_bias': dt_bias,
        'ssd_a_log': a_log,
        'ssd_d': gain((L, SSD_HEADS)),
        'ssd_norm': gain((L, SSD_D_INNER)),
        'ssd_w_o': nrm((L, SSD_D_INNER, D_MODEL), SSD_D_INNER ** -0.5),
        'w_out': nrm((L, D_MODEL, D_MODEL), D_MODEL ** -0.5),
        'norm_ffn2': gain((L, D_MODEL)),
        'ffn2_w_gate': nrm((L, D_MODEL, D_FF), D_MODEL ** -0.5),
        'ffn2_w_up': nrm((L, D_MODEL, D_FF), D_MODEL ** -0.5),
        'ffn2_w_down': nrm((L, D_FF, D_MODEL), D_FF ** -0.5),
    }


def reference(x, c, ctx, c_ctx, w_ada, b_ada, norm_ffn1, ffn1_w_gate, ffn1_w_up, ffn1_w_down,
              norm_mix, w_in, q_norm, k_norm, na_rpb, na_w_o, ssd_conv_w, ssd_conv_b,
              ssd_dt_bias, ssd_a_log, ssd_d, ssd_norm, ssd_w_o, w_out,
              norm_ffn2, ffn2_w_gate, ffn2_w_up, ffn2_w_down):
    cos, sin = rope_2d(x.shape[1])
    for l in range(DEPTH):
        need_ctx = l < DEPTH - 1
        mod_l = adaln(c, w_ada[l], b_ada[l])[:, None]
        mod_c = adaln(c_ctx, w_ada[l], b_ada[l])[None, None]
        x = ffn_sublayer(x, mod_l, 0, norm_ffn1[l], ffn1_w_gate[l], ffn1_w_up[l], ffn1_w_down[l])
        ctx = ffn_sublayer(ctx, mod_c, 0, norm_ffn1[l], ffn1_w_gate[l], ffn1_w_up[l], ffn1_w_down[l])
        h_l = modulate(rmsnorm(x, norm_mix[l]), mod_l[:, :, 3], mod_l[:, :, 4])
        h_c = modulate(rmsnorm(ctx, norm_mix[l]), mod_c[:, :, 3], mod_c[:, :, 4])
        y_l, y_c = hybrid_mixer(h_l, h_c, w_in[l], q_norm[l], k_norm[l], na_rpb[l], na_w_o[l],
                                ssd_conv_w[l], ssd_conv_b[l], ssd_dt_bias[l], ssd_a_log[l], ssd_d[l],
                                ssd_norm[l], ssd_w_o[l], w_out[l], cos, sin, need_ctx)
        x = x + mod_l[:, :, 5] * y_l
        x = ffn_sublayer(x, mod_l, 6, norm_ffn2[l], ffn2_w_gate[l], ffn2_w_up[l], ffn2_w_down[l])
        if need_ctx:
            ctx = ctx + mod_c[:, :, 5] * y_c
            ctx = ffn_sublayer(ctx, mod_c, 6, norm_ffn2[l], ffn2_w_gate[l], ffn2_w_up[l], ffn2_w_down[l])
    return x
```

```python
import functools

import numpy as np
import jax
import jax.numpy as jnp
from jax import lax
from jax.experimental import pallas as pl
from jax.experimental.pallas import tpu as pltpu

F32 = jnp.float32
BF16 = jnp.bfloat16

D_MODEL = 1024
GRID_W = 64
HEAD_DIM = 64
NA_HEADS = 16
NA_WIDTH = NA_HEADS * HEAD_DIM
WIN_R = 8
WIN_C = 16
ROPE_BASE = 10000.0
ATTN_SCALE = HEAD_DIM ** -0.5
SSD_D_INNER = 2 * D_MODEL
SSD_HEADDIM = 64
SSD_HEADS = SSD_D_INNER // SSD_HEADDIM
SSD_GROUPS = 4
SSD_STATE = 128
SSD_CONV = 5
SSD_CHUNK = 128
SSD_BC = SSD_GROUPS * SSD_STATE
SSD_CONV_DIM = SSD_D_INNER + 2 * SSD_BC
D_FF = 2816
N_MOD = 9
EPS = 1e-6

V7X_LANES = 128
V7X_SUBLANES = 8
V7X_MXU_DIM = 256
V7X_VMEM_LIMIT_BYTES = 56 * 1024 * 1024

HEAD_PAIR = V7X_LANES // HEAD_DIM
N_PAIRS = NA_HEADS // HEAD_PAIR
SSD_PAIRS = SSD_HEADS // HEAD_PAIR
PAIRS_PER_GROUP = SSD_PAIRS // SSD_GROUPS
DT_PAD = V7X_LANES
MASK_BIAS = -1e30

TM_FFN = 512
TM_PROJ = 256
TM_MERGE = 512
FF_CHUNK = V7X_MXU_DIM
QK_CHUNK = V7X_MXU_DIM
PROJ_CHUNK = 512
CONV_COL_CHUNK = 512

OFF_Q = 0
OFF_K = OFF_Q + NA_WIDTH
OFF_V = OFF_K + NA_WIDTH
OFF_Z = OFF_V + NA_WIDTH
OFF_XBC = OFF_Z + SSD_D_INNER
OFF_G = OFF_XBC + SSD_CONV_DIM
OFF_DT = OFF_G + 2 * D_MODEL
N_IN_PACKED = OFF_DT + DT_PAD


def _cparams(n_axes):
    return pltpu.CompilerParams(dimension_semantics=("arbitrary",) * n_axes,
                                vmem_limit_bytes=V7X_VMEM_LIMIT_BYTES)


def _resident():
    return pl.BlockSpec(memory_space=pltpu.VMEM)


def _sigmoid(x):
    return 1.0 / (1.0 + jnp.exp(-x))


def _silu(x):
    return x * _sigmoid(x)


def _softplus(x):
    return jnp.maximum(x, 0.0) + jnp.log(1.0 + jnp.exp(-jnp.abs(x)))


def _dot(a, b):
    return jnp.dot(a, b, preferred_element_type=F32)


def _dot_nt(a, b):
    return lax.dot_general(a, b, (((1,), (1,)), ((), ())), preferred_element_type=F32)


def _norm_modulate(x, g_row, shift, scale):
    ms = jnp.mean(x * x, axis=-1, keepdims=True)
    y = x * lax.rsqrt(ms + EPS) * g_row
    return y * (1.0 + scale) + shift


def _adaln_kernel(v_ref, w_ref, b_ref, o_ref):
    sv = _silu(v_ref[...]).astype(BF16)
    o_ref[0] = _dot(sv, w_ref[0].astype(BF16)) + b_ref[0]


def _adaln(v_rows, w_ada, b_ada):
    depth, d, n = w_ada.shape
    rows = v_rows.shape[0]
    tn = D_MODEL
    return pl.pallas_call(
        _adaln_kernel,
        grid=(depth, n // tn),
        in_specs=[pl.BlockSpec((rows, d), lambda l, j: (0, 0)),
                  pl.BlockSpec((1, d, tn), lambda l, j: (l, 0, j)),
                  pl.BlockSpec((1, 1, tn), lambda l, j: (l, 0, j))],
        out_specs=pl.BlockSpec((1, rows, tn), lambda l, j: (l, 0, j)),
        out_shape=jax.ShapeDtypeStruct((depth, rows, n), F32),
        compiler_params=_cparams(2),
        name="adaln",
    )(v_rows, w_ada, b_ada.reshape(depth, 1, n))


def _ffn_kernel(s_ref, mod_ref, g_ref, wg_ref, wu_ref, wd_ref, o_ref, act_scr, *, mod_i):
    x = s_ref[...]
    shift = mod_ref[0, mod_i:mod_i + 1, :]
    scale = mod_ref[0, mod_i + 1:mod_i + 2, :]
    gate = mod_ref[0, mod_i + 2:mod_i + 3, :]
    h = _norm_modulate(x, g_ref[...], shift, scale).astype(BF16)
    for j in range(D_FF // FF_CHUNK):
        cs = slice(j * FF_CHUNK, (j + 1) * FF_CHUNK)
        act_scr[:, cs] = (_silu(_dot(h, wg_ref[:, cs])) * _dot(h, wu_ref[:, cs])).astype(BF16)
    y = _dot(act_scr[...], wd_ref[...])
    o_ref[...] = x + 0.5 * gate * y


def _ffn(s, n_rows, mod, g_norm, wg, wu, wd, *, mod_i, seq, batch):
    tm = TM_FFN
    return pl.pallas_call(
        functools.partial(_ffn_kernel, mod_i=mod_i),
        grid=(n_rows // tm,),
        in_specs=[pl.BlockSpec((tm, D_MODEL), lambda i: (i, 0)),
                  pl.BlockSpec((1, N_MOD, D_MODEL), lambda i: (jnp.minimum(i * tm // seq, batch), 0, 0)),
                  pl.BlockSpec((1, D_MODEL), lambda i: (0, 0)),
                  _resident(), _resident(), _resident()],
        out_specs=pl.BlockSpec((tm, D_MODEL), lambda i: (i, 0)),
        out_shape=jax.ShapeDtypeStruct((n_rows, D_MODEL), F32),
        scratch_shapes=[pltpu.VMEM((tm, D_FF), BF16)],
        compiler_params=_cparams(1),
        name="ffn",
    )(s, mod, g_norm.reshape(1, D_MODEL), wg, wu, wd)


def _proj_kernel(s_ref, mod_ref, g_ref, w_ref, gmat_ref, qg_ref, kg_ref, cos_ref, sin_ref,
                 q_ref, k_ref, v_ref, z_ref, xbc_ref, gg_ref, dt_ref, h_scr):
    x = s_ref[...]
    h_scr[...] = _norm_modulate(x, g_ref[...], mod_ref[0, 3:4, :], mod_ref[0, 4:5, :]).astype(BF16)
    tm = x.shape[0]
    lane = lax.broadcasted_iota(jnp.int32, (tm, QK_CHUNK), 1)
    first_half = (lane % HEAD_DIM) < (HEAD_DIM // 2)
    cos = cos_ref[...]
    sin = sin_ref[...]

    def qk_epilogue(off, gain_ref, out_ref):
        for j in range(NA_WIDTH // QK_CHUNK):
            cs = slice(j * QK_CHUNK, (j + 1) * QK_CHUNK)
            y = _dot(h_scr[...], w_ref[:, off + j * QK_CHUNK:off + (j + 1) * QK_CHUNK])
            sq = y * y
            hi = sq.astype(BF16)
            lo = (sq - hi.astype(F32)).astype(BF16)
            ms = _dot(hi, gmat_ref[...]) + _dot(lo, gmat_ref[...])
            yn = y * lax.rsqrt(ms + EPS) * gain_ref[:, cs]
            swapped = jnp.where(first_half,
                                pltpu.roll(yn, QK_CHUNK - HEAD_DIM // 2, 1),
                                pltpu.roll(yn, HEAD_DIM // 2, 1))
            out_ref[:, cs] = (yn * cos + swapped * sin).astype(out_ref.dtype)

    qk_epilogue(OFF_Q, qg_ref, q_ref)
    qk_epilogue(OFF_K, kg_ref, k_ref)

    def plain(off, width, out_ref):
        for j in range(width // PROJ_CHUNK):
            y = _dot(h_scr[...], w_ref[:, off + j * PROJ_CHUNK:off + (j + 1) * PROJ_CHUNK])
            out_ref[:, j * PROJ_CHUNK:(j + 1) * PROJ_CHUNK] = y.astype(out_ref.dtype)

    plain(OFF_V, NA_WIDTH, v_ref)
    plain(OFF_Z, SSD_D_INNER, z_ref)
    plain(OFF_XBC, SSD_CONV_DIM, xbc_ref)
    plain(OFF_G, 2 * D_MODEL, gg_ref)
    dt_ref[...] = _dot(h_scr[...], w_ref[:, OFF_DT:OFF_DT + DT_PAD])


def _proj(s, mod, g_norm, w_packed, gmat, q_gain, k_gain, cos_tab, sin_tab, *, seq, batch):
    n_rows = s.shape[0]
    tm = TM_PROJ
    n_lat = batch * seq // tm
    tiles_per_seq = seq // tm

    def rope_idx(i):
        return (jnp.where(i < n_lat, i % tiles_per_seq, tiles_per_seq), 0)

    def rows(width):
        return pl.BlockSpec((tm, width), lambda i: (i, 0))

    def out(width, dtype):
        return jax.ShapeDtypeStruct((n_rows, width), dtype)

    return pl.pallas_call(
        _proj_kernel,
        grid=(n_rows // tm,),
        in_specs=[rows(D_MODEL),
                  pl.BlockSpec((1, N_MOD, D_MODEL), lambda i: (jnp.minimum(i * tm // seq, batch), 0, 0)),
                  pl.BlockSpec((1, D_MODEL), lambda i: (0, 0)),
                  _resident(), _resident(),
                  pl.BlockSpec((1, NA_WIDTH), lambda i: (0, 0)),
                  pl.BlockSpec((1, NA_WIDTH), lambda i: (0, 0)),
                  pl.BlockSpec((tm, QK_CHUNK), rope_idx),
                  pl.BlockSpec((tm, QK_CHUNK), rope_idx)],
        out_specs=[rows(NA_WIDTH), rows(NA_WIDTH), rows(NA_WIDTH), rows(SSD_D_INNER),
                   rows(SSD_CONV_DIM), rows(2 * D_MODEL), rows(DT_PAD)],
        out_shape=[out(NA_WIDTH, BF16), out(NA_WIDTH, BF16), out(NA_WIDTH, BF16), out(SSD_D_INNER, F32),
                   out(SSD_CONV_DIM, F32), out(2 * D_MODEL, F32), out(DT_PAD, F32)],
        scratch_shapes=[pltpu.VMEM((tm, D_MODEL), BF16)],
        compiler_params=_cparams(1),
        name="in_proj",
    )(s, mod, g_norm.reshape(1, D_MODEL), w_packed, gmat, q_gain, k_gain, cos_tab, sin_tab)


def _attn_kernel(q_ref, k_ref, v_ref, kc_ref, vc_ref, bias_ref, o_ref, *, n_rg, rows_per_step, grid_rows,
                 with_ctx):
    rg = pl.program_id(2)
    lane = lax.broadcasted_iota(jnp.int32, (GRID_W, V7X_LANES), 1)
    low = lane < HEAD_DIM
    win_keys = WIN_R * GRID_W

    def one_row(rr, kw, vw, bias):
        qrow = q_ref[rr * GRID_W:(rr + 1) * GRID_W, :]
        zero = jnp.zeros_like(qrow)
        qs = jnp.concatenate([jnp.where(low, qrow, zero), jnp.where(low, zero, qrow)], axis=0)
        sc = _dot_nt(qs, kc_ref[...])
        m = jnp.max(sc, axis=-1, keepdims=True)
        if kw is not None:
            sw = _dot_nt(qs, kw) + bias
            m = jnp.maximum(m, jnp.max(sw, axis=-1, keepdims=True))
            pw = jnp.exp(sw - m)
        pc = jnp.exp(sc - m)
        denom = jnp.sum(pc, axis=-1, keepdims=True)
        o = _dot(pc.astype(BF16), vc_ref[...])
        if kw is not None:
            denom = denom + jnp.sum(pw, axis=-1, keepdims=True)
            o = o + _dot(pw.astype(BF16), vw)
        o = o / denom
        o_ref[rr * GRID_W:(rr + 1) * GRID_W, :] = jnp.where(low, o[:GRID_W], o[GRID_W:]).astype(o_ref.dtype)

    def latent():
        for rr in range(rows_per_step):
            r = rg * rows_per_step + rr
            r0 = jnp.clip(r - WIN_R // 2, 0, grid_rows - WIN_R)
            start = pl.multiple_of(r0 * GRID_W, GRID_W)
            one_row(rr, k_ref[pl.ds(start, win_keys), :], v_ref[pl.ds(start, win_keys), :], bias_ref[0, r - r0])

    if with_ctx:
        pl.when(rg < n_rg)(latent)

        @pl.when(rg == n_rg)
        def _():
            for rr in range(rows_per_step):
                one_row(rr, None, None, None)
    else:
        latent()


def _attention(q, k, v, bias, *, seq, ctx_len, batch, with_ctx):
    n_rows = q.shape[0]
    rows_per_step = ctx_len // GRID_W
    tq = rows_per_step * GRID_W
    grid_rows = seq // GRID_W
    n_rg = grid_rows // rows_per_step
    ctx_blk0 = batch * seq // ctx_len

    def q_idx(j, b, rg):
        return (jnp.where(rg < n_rg, b * n_rg + rg, batch * n_rg + b), j)

    out_rows = n_rows if with_ctx else batch * seq
    return pl.pallas_call(
        functools.partial(_attn_kernel, n_rg=n_rg, rows_per_step=rows_per_step, grid_rows=grid_rows,
                          with_ctx=with_ctx),
        grid=(N_PAIRS, batch, n_rg + (1 if with_ctx else 0)),
        in_specs=[pl.BlockSpec((tq, V7X_LANES), q_idx),
                  pl.BlockSpec((seq, V7X_LANES), lambda j, b, rg: (b, j)),
                  pl.BlockSpec((seq, V7X_LANES), lambda j, b, rg: (b, j)),
                  pl.BlockSpec((ctx_len, V7X_LANES), lambda j, b, rg: (ctx_blk0 + b, j)),
                  pl.BlockSpec((ctx_len, V7X_LANES), lambda j, b, rg: (ctx_blk0 + b, j)),
                  pl.BlockSpec((1, WIN_R, V7X_LANES, WIN_R * GRID_W), lambda j, b, rg: (j, 0, 0, 0))],
        out_specs=pl.BlockSpec((tq, V7X_LANES), q_idx),
        out_shape=jax.ShapeDtypeStruct((out_rows, NA_WIDTH), BF16),
        compiler_params=_cparams(3),
        name="attention",
    )(q, k, v, k, v, bias)


def _attention_bias(rpb):
    off = np.arange(WIN_R)[:, None]
    i = np.arange(WIN_R)[None, :]
    idx_r = i - off + (WIN_R - 1)
    c = np.arange(GRID_W)[:, None]
    kc = np.arange(GRID_W)[None, :]
    s_col = np.clip(c - WIN_C // 2, 0, GRID_W - WIN_C)
    valid = (kc >= s_col) & (kc < s_col + WIN_C)
    idx_c = np.clip(kc - c, -(WIN_C - 1), WIN_C - 1) + (WIN_C - 1)
    tbl = rpb[:, idx_r[:, :, None, None], idx_c[None, None, :, :]]
    tbl = jnp.where(valid[None, None, None], tbl.astype(F32), MASK_BIAS)
    tbl = jnp.transpose(tbl, (0, 1, 3, 2, 4)).reshape(N_PAIRS, HEAD_PAIR, WIN_R, GRID_W, WIN_R * GRID_W)
    return jnp.transpose(tbl, (0, 2, 1, 3, 4)).reshape(N_PAIRS, WIN_R, HEAD_PAIR * GRID_W, WIN_R * GRID_W)


def _scan_rows(x, reverse):
    n = x.shape[0]
    row = lax.broadcasted_iota(jnp.int32, x.shape, 0)
    step = 1
    while step < n:
        if reverse:
            x = x + jnp.where(row < n - step, pltpu.roll(x, n - step, 0), 0.0)
        else:
            x = x + jnp.where(row >= step, pltpu.roll(x, step, 0), 0.0)
        step *= 2
    return x


def _ssd_chunk(u_ref, dt_ref, dtb_ref, alog_ref, st_scr, write_y, *, reverse):
    q = SSD_CHUNK
    direction = 1 if reverse else 0
    dt = _softplus(dt_ref[...] + dtb_ref[...])
    a = -jnp.exp(alog_ref[...])
    ac = _scan_rows(dt * a, reverse)
    edge = 0 if reverse else q - 1
    wdt = jnp.exp(ac[edge:edge + 1, :] - ac) * dt
    ac_t = ac.T
    dt_t = dt.T
    wdt_t = wdt.T
    ii = lax.broadcasted_iota(jnp.int32, (q, q), 0)
    jj = lax.broadcasted_iota(jnp.int32, (q, q), 1)
    causal = (ii <= jj) if reverse else (ii >= jj)
    low = jj < SSD_HEADDIM
    for g in range(SSD_GROUPS):
        b_g = u_ref[:, SSD_D_INNER + g * SSD_STATE:SSD_D_INNER + (g + 1) * SSD_STATE]
        c_g = u_ref[:, SSD_D_INNER + SSD_BC + g * SSD_STATE:SSD_D_INNER + SSD_BC + (g + 1) * SSD_STATE]
        cb = _dot_nt(c_g.astype(BF16), b_g.astype(BF16))
        b_t = b_g.T
        for pp in range(PAIRS_PER_GROUP):
            pair = g * PAIRS_PER_GROUP + pp
            xp = u_ref[:, pair * V7X_LANES:(pair + 1) * V7X_LANES].astype(BF16)
            st = st_scr[pair]
            rhs = jnp.concatenate([xp, st.astype(BF16)], axis=0)
            ys = []
            sts = []
            for half in range(HEAD_PAIR):
                hl = direction * SSD_HEADS + pair * HEAD_PAIR + half
                a_col = jnp.broadcast_to(ac[:, hl:hl + 1], (q, q))
                e_col = jnp.exp(a_col)
                decay = jnp.exp(jnp.where(causal, a_col - ac_t[hl:hl + 1, :], -jnp.inf))
                m = decay * cb * dt_t[hl:hl + 1, :]
                lhs = jnp.concatenate([m, c_g * e_col], axis=1).astype(BF16)
                ys.append(_dot(lhs, rhs))
                bw = (b_t * wdt_t[hl:hl + 1, :]).astype(BF16)
                sts.append(e_col[edge:edge + 1, :] * st + _dot(bw, xp))
            write_y(pair, jnp.where(low, ys[0], ys[1]))
            st_scr[pair] = jnp.where(low, sts[0], sts[1])


def _ssd_position(step, b, *, n_cc, n_lc, batch, reverse):
    is_ctx = step < n_cc
    loc = jnp.where(is_ctx, step, step - n_cc)
    if reverse:
        loc = jnp.where(is_ctx, n_cc - 1 - loc, n_lc - 1 - loc)
    blk = jnp.where(is_ctx, batch * n_lc + b * n_cc + loc, b * n_lc + loc)
    n_here = jnp.where(is_ctx, n_cc, n_lc)
    return blk, loc, n_here


def _ssd_fwd_kernel(xbc_ref, prev_ref, next_ref, dt_ref, cw_ref, cbias_ref, dtb_ref, alog_ref,
                    u_ref, y_ref, xs_scr, st_scr, *, n_cc, n_lc, batch):
    step = pl.program_id(1)

    @pl.when(step == 0)
    def _():
        st_scr[...] = jnp.zeros_like(st_scr)

    _, loc, n_here = _ssd_position(step, pl.program_id(0), n_cc=n_cc, n_lc=n_lc, batch=batch, reverse=False)
    halo = V7X_SUBLANES
    q = SSD_CHUNK
    xs_scr[0:halo, :] = jnp.where(loc == 0, 0.0, prev_ref[...])
    xs_scr[halo:halo + q, :] = xbc_ref[...]
    xs_scr[halo + q:2 * halo + q, :] = jnp.where(loc == n_here - 1, 0.0, next_ref[...])
    for j in range(SSD_CONV_DIM // CONV_COL_CHUNK):
        cs = slice(j * CONV_COL_CHUNK, (j + 1) * CONV_COL_CHUNK)
        acc = jnp.broadcast_to(cbias_ref[:, cs], (q, CONV_COL_CHUNK))
        for t in range(SSD_CONV):
            r0 = halo - SSD_CONV // 2 + t
            acc = acc + xs_scr[r0:r0 + q, cs] * cw_ref[t:t + 1, cs]
        u_ref[:, cs] = _silu(acc)

    def write_y(pair, val):
        y_ref[:, pair * V7X_LANES:(pair + 1) * V7X_LANES] = val

    _ssd_chunk(u_ref, dt_ref, dtb_ref, alog_ref, st_scr, write_y, reverse=False)


def _ssd_bwd_kernel(u_ref, dt_ref, z_ref, yf_ref, dtb_ref, alog_ref, dskip_ref, nw_ref,
                    so_ref, y_scr, st_scr):
    step = pl.program_id(1)

    @pl.when(step == 0)
    def _():
        st_scr[...] = jnp.zeros_like(st_scr)

    def write_y(pair, val):
        y_scr[:, pair * V7X_LANES:(pair + 1) * V7X_LANES] = val

    _ssd_chunk(u_ref, dt_ref, dtb_ref, alog_ref, st_scr, write_y, reverse=True)
    gw = SSD_D_INNER // SSD_GROUPS
    for g in range(SSD_GROUPS):
        cs = slice(g * gw, (g + 1) * gw)
        y = yf_ref[:, cs] + y_scr[:, cs] + dskip_ref[:, cs] * u_ref[:, cs]
        y = y * _silu(z_ref[:, cs])
        ms = jnp.mean(y * y, axis=-1, keepdims=True)
        so_ref[:, cs] = (y * lax.rsqrt(ms + EPS) * nw_ref[:, cs]).astype(so_ref.dtype)


def _ssd(xbc, dt, z, conv_w, conv_b, dt_bias, a_log, d_skip, norm_w, *, seq, ctx_len, batch):
    n_rows = xbc.shape[0]
    q = SSD_CHUNK
    n_cc, n_lc = ctx_len // q, seq // q
    steps = n_cc + n_lc
    halo = V7X_SUBLANES
    halo_per_chunk = q // halo
    n_halo_blocks = n_rows // halo

    def pos(reverse):
        return functools.partial(_ssd_position, n_cc=n_cc, n_lc=n_lc, batch=batch, reverse=reverse)

    def chunk_rows(width, reverse):
        return pl.BlockSpec((q, width), lambda b, s: (pos(reverse)(s, b)[0], 0))

    def small(rows, width):
        return pl.BlockSpec((rows, width), lambda b, s: (0, 0))

    prev_spec = pl.BlockSpec(
        (halo, SSD_CONV_DIM), lambda b, s: (jnp.maximum(pos(False)(s, b)[0] * halo_per_chunk - 1, 0), 0))
    next_spec = pl.BlockSpec(
        (halo, SSD_CONV_DIM),
        lambda b, s: (jnp.minimum((pos(False)(s, b)[0] + 1) * halo_per_chunk, n_halo_blocks - 1), 0))

    conv_w_rows = jnp.zeros((V7X_SUBLANES, SSD_CONV_DIM), F32).at[:SSD_CONV].set(conv_w)
    pad = DT_PAD - 2 * SSD_HEADS
    dtb_row = jnp.pad(dt_bias.reshape(1, 2 * SSD_HEADS), ((0, 0), (0, pad)))
    alog_row = jnp.pad(a_log.reshape(1, 2 * SSD_HEADS), ((0, 0), (0, pad)))
    state = pltpu.VMEM((SSD_PAIRS, SSD_STATE, V7X_LANES), F32)

    u, y_f = pl.pallas_call(
        functools.partial(_ssd_fwd_kernel, n_cc=n_cc, n_lc=n_lc, batch=batch),
        grid=(batch, steps),
        in_specs=[chunk_rows(SSD_CONV_DIM, False), prev_spec, next_spec, chunk_rows(DT_PAD, False),
                  small(V7X_SUBLANES, SSD_CONV_DIM), small(1, SSD_CONV_DIM), small(1, DT_PAD), small(1, DT_PAD)],
        out_specs=[chunk_rows(SSD_CONV_DIM, False), chunk_rows(SSD_D_INNER, False)],
        out_shape=[jax.ShapeDtypeStruct((n_rows, SSD_CONV_DIM), F32),
                   jax.ShapeDtypeStruct((n_rows, SSD_D_INNER), F32)],
        scratch_shapes=[pltpu.VMEM((q + 2 * halo, SSD_CONV_DIM), F32), state],
        compiler_params=_cparams(2),
        name="ssd_fwd",
    )(xbc, xbc, xbc, dt, conv_w_rows, conv_b.reshape(1, SSD_CONV_DIM), dtb_row, alog_row)

    d_row = jnp.repeat(d_skip, SSD_HEADDIM).reshape(1, SSD_D_INNER)
    return pl.pallas_call(
        _ssd_bwd_kernel,
        grid=(batch, steps),
        in_specs=[chunk_rows(SSD_CONV_DIM, True), chunk_rows(DT_PAD, True), chunk_rows(SSD_D_INNER, True),
                  chunk_rows(SSD_D_INNER, True), small(1, DT_PAD), small(1, DT_PAD),
                  small(1, SSD_D_INNER), small(1, SSD_D_INNER)],
        out_specs=chunk_rows(SSD_D_INNER, True),
        out_shape=jax.ShapeDtypeStruct((n_rows, SSD_D_INNER), BF16),
        scratch_shapes=[pltpu.VMEM((q, SSD_D_INNER), F32), state],
        compiler_params=_cparams(2),
        name="ssd_bwd",
    )(u, dt, z, y_f, dtb_row, alog_row, d_row, norm_w.reshape(1, SSD_D_INNER))


def _merge_kernel(s_ref, mod_ref, a_ref, so_ref, g_ref, wa_ref, ws_ref, wo_ref, o_ref):
    ya = _dot(a_ref[...], wa_ref[...])
    ys = _dot(so_ref[...], ws_ref[...])
    g = _sigmoid(g_ref[...])
    merged = (g[:, :D_MODEL] * ya + g[:, D_MODEL:] * ys).astype(BF16)
    o_ref[...] = s_ref[...] + mod_ref[0, 5:6, :] * _dot(merged, wo_ref[...])


def _merge(s, n_rows, mod, a_o, s_o, g, wa, ws, wo, *, seq, batch):
    tm = TM_MERGE

    def rows(width):
        return pl.BlockSpec((tm, width), lambda i: (i, 0))

    return pl.pallas_call(
        _merge_kernel,
        grid=(n_rows // tm,),
        in_specs=[rows(D_MODEL),
                  pl.BlockSpec((1, N_MOD, D_MODEL), lambda i: (jnp.minimum(i * tm // seq, batch), 0, 0)),
                  rows(NA_WIDTH), rows(SSD_D_INNER), rows(2 * D_MODEL),
                  _resident(), _resident(), _resident()],
        out_specs=rows(D_MODEL),
        out_shape=jax.ShapeDtypeStruct((n_rows, D_MODEL), F32),
        compiler_params=_cparams(1),
        name="merge",
    )(s, mod, a_o, s_o, g, wa, ws, wo)


def _rope_tables(seq, pad_rows):
    t = np.arange(seq)
    row = (t // GRID_W).astype(np.float32)
    col = (t % GRID_W).astype(np.float32)
    n_freq = HEAD_DIM // 4
    inv = jnp.asarray(ROPE_BASE, F32) ** (-jnp.arange(n_freq, dtype=F32) / n_freq)
    ang = jnp.concatenate([jnp.asarray(row)[:, None] * inv, jnp.asarray(col)[:, None] * inv], axis=-1)
    cos, sin = jnp.cos(ang), jnp.sin(ang)
    reps = QK_CHUNK // HEAD_DIM
    cos_t = jnp.tile(jnp.concatenate([cos, cos], axis=-1), (1, reps))
    sin_t = jnp.tile(jnp.concatenate([-sin, sin], axis=-1), (1, reps))
    cos_t = jnp.concatenate([cos_t, jnp.ones((pad_rows, QK_CHUNK), F32)], axis=0)
    sin_t = jnp.concatenate([sin_t, jnp.zeros((pad_rows, QK_CHUNK), F32)], axis=0)
    return cos_t, sin_t


def _pack_w_in(w_in_l):
    sizes = (NA_WIDTH, NA_WIDTH, NA_WIDTH, SSD_D_INNER, SSD_CONV_DIM, 2 * SSD_HEADS, 2 * D_MODEL)
    offs = np.concatenate([[0], np.cumsum(sizes)])
    q, k, v, z, xbc, dt, g = (w_in_l[:, offs[i]:offs[i + 1]] for i in range(7))
    pad = jnp.zeros((D_MODEL, DT_PAD - 2 * SSD_HEADS), w_in_l.dtype)
    return jnp.concatenate([q, k, v, z, xbc, g, dt, pad], axis=1).astype(BF16)


def kernel(x, c, ctx, c_ctx, w_ada, b_ada, norm_ffn1, ffn1_w_gate, ffn1_w_up, ffn1_w_down, norm_mix, w_in,
           q_norm, k_norm, na_rpb, na_w_o, ssd_conv_w, ssd_conv_b, ssd_dt_bias, ssd_a_log, ssd_d, ssd_norm,
           ssd_w_o, w_out, norm_ffn2, ffn2_w_gate, ffn2_w_up, ffn2_w_down):
    batch, seq, d = x.shape
    ctx_len = ctx.shape[1]
    depth = w_ada.shape[0]
    n_lat = batch * seq
    n_all = n_lat + batch * ctx_len
    assert d == D_MODEL and seq % GRID_W == 0 and seq // GRID_W >= WIN_R
    assert seq % TM_FFN == 0 and (batch * ctx_len) % TM_FFN == 0 and seq % TM_PROJ == 0
    assert ctx_len % SSD_CHUNK == 0 and seq % ctx_len == 0 and ctx_len % GRID_W == 0

    s = jnp.concatenate([x.reshape(n_lat, d), ctx.reshape(batch * ctx_len, d)], axis=0)

    mod_rows = -(-(batch + 1) // V7X_SUBLANES) * V7X_SUBLANES
    v_rows = jnp.zeros((mod_rows, d), F32).at[:batch].set(c).at[batch].set(c_ctx)
    mod_all = _adaln(v_rows, w_ada, b_ada).reshape(depth, mod_rows, N_MOD, d)

    cos_tab, sin_tab = _rope_tables(seq, TM_PROJ)
    gmat = jnp.asarray(np.kron(np.eye(QK_CHUNK // HEAD_DIM), np.full((HEAD_DIM, HEAD_DIM), 1.0 / HEAD_DIM)), BF16)

    for l in range(depth):
        need_ctx = l < depth - 1
        mod = mod_all[l]
        s = _ffn(s, n_all, mod, norm_ffn1[l], ffn1_w_gate[l].astype(BF16), ffn1_w_up[l].astype(BF16),
                 ffn1_w_down[l].astype(BF16), mod_i=0, seq=seq, batch=batch)
        q_gain = jnp.tile(q_norm[l] * ATTN_SCALE, NA_HEADS).reshape(1, NA_WIDTH)
        k_gain = jnp.tile(k_norm[l], NA_HEADS).reshape(1, NA_WIDTH)
        q, k, v, z, xbc, g, dt = _proj(s, mod, norm_mix[l], _pack_w_in(w_in[l]), gmat, q_gain, k_gain,
                                       cos_tab, sin_tab, seq=seq, batch=batch)
        a_o = _attention(q, k, v, _attention_bias(na_rpb[l]), seq=seq, ctx_len=ctx_len, batch=batch,
                         with_ctx=need_ctx)
        s_o = _ssd(xbc, dt, z, ssd_conv_w[l], ssd_conv_b[l], ssd_dt_bias[l], ssd_a_log[l], ssd_d[l],
                   ssd_norm[l], seq=seq, ctx_len=ctx_len, batch=batch)
        n_out = n_all if need_ctx else n_lat
        s = _merge(s, n_out, mod, a_o, s_o, g, na_w_o[l].astype(BF16), ssd_w_o[l].astype(BF16),
                   w_out[l].astype(BF16), seq=seq, batch=batch)
        s = _ffn(s, n_out, mod, norm_ffn2[l], ffn2_w_gate[l].astype(BF16), ffn2_w_up[l].astype(BF16),
                 ffn2_w_down[l].astype(BF16), mod_i=6, seq=seq, batch=batch)
    return s[:n_lat].reshape(batch, seq, d)
```

```python
import functools

import numpy as np
import jax
import jax.numpy as jnp
from jax import lax
from jax.experimental import pallas as pl
from jax.experimental.pallas import tpu as pltpu

F32 = jnp.float32
BF16 = jnp.bfloat16

D_MODEL = 1024
GRID_W = 64
HEAD_DIM = 64
NA_HEADS = 16
NA_WIDTH = NA_HEADS * HEAD_DIM
WIN_R = 8
WIN_C = 16
ROPE_BASE = 10000.0
ATTN_SCALE = HEAD_DIM ** -0.5
SSD_D_INNER = 2 * D_MODEL
SSD_HEADDIM = 64
SSD_HEADS = SSD_D_INNER // SSD_HEADDIM
SSD_GROUPS = 4
SSD_STATE = 128
SSD_CONV = 5
SSD_CHUNK = 128
SSD_BC = SSD_GROUPS * SSD_STATE
SSD_CONV_DIM = SSD_D_INNER + 2 * SSD_BC
D_FF = 2816
N_MOD = 9
EPS = 1e-6

V7X_LANES = 128
V7X_SUBLANES = 8
V7X_MXU_DIM = 256
V7X_VMEM_LIMIT_BYTES = 56 * 1024 * 1024

HEAD_PAIR = V7X_LANES // HEAD_DIM
N_PAIRS = NA_HEADS // HEAD_PAIR
SSD_PAIRS = SSD_HEADS // HEAD_PAIR
PAIRS_PER_GROUP = SSD_PAIRS // SSD_GROUPS
DT_PAD = V7X_LANES
MASK_BIAS = -1e30
LOG2_E = 1.4426950408889634

TM_FFN = 512
TM_PROJ = 256
TM_MERGE = 512
ATTN_ROWS_PER_STEP = 8
FF_CHUNK = V7X_MXU_DIM
QK_CHUNK = V7X_MXU_DIM
PROJ_CHUNK = 512

OFF_Q = 0
OFF_K = OFF_Q + NA_WIDTH
OFF_V = OFF_K + NA_WIDTH
OFF_Z = OFF_V + NA_WIDTH
OFF_XBC = OFF_Z + SSD_D_INNER
OFF_G = OFF_XBC + SSD_CONV_DIM
OFF_DT = OFF_G + 2 * D_MODEL
N_IN_PACKED = OFF_DT + DT_PAD


def _cparams(n_axes):
    return pltpu.CompilerParams(dimension_semantics=("arbitrary",) * n_axes,
                                vmem_limit_bytes=V7X_VMEM_LIMIT_BYTES)


def _resident():
    return pl.BlockSpec(memory_space=pltpu.VMEM)


def _sigmoid(x):
    return 0.5 * jnp.tanh(0.5 * x) + 0.5


def _silu(x):
    return x * _sigmoid(x)


def _softplus(x):
    return jnp.maximum(x, 0.0) + jnp.log(1.0 + jnp.exp(-jnp.abs(x)))


def _dot(a, b):
    return jnp.dot(a, b, preferred_element_type=F32)


def _dot_nt(a, b):
    return lax.dot_general(a, b, (((1,), (1,)), ((), ())), preferred_element_type=F32)


def _norm_modulate(x, g_row, shift, scale):
    ms = jnp.mean(x * x, axis=-1, keepdims=True)
    y = x * lax.rsqrt(ms + EPS) * g_row
    return y * (1.0 + scale) + shift


def _adaln_kernel(v_ref, w_ref, b_ref, o_ref):
    sv = _silu(v_ref[...]).astype(BF16)
    o_ref[0] = _dot(sv, w_ref[0].astype(BF16)) + b_ref[0]


def _adaln(v_rows, w_ada, b_ada):
    depth, d, n = w_ada.shape
    rows = v_rows.shape[0]
    tn = D_MODEL
    return pl.pallas_call(
        _adaln_kernel,
        grid=(depth, n // tn),
        in_specs=[pl.BlockSpec((rows, d), lambda l, j: (0, 0)),
                  pl.BlockSpec((1, d, tn), lambda l, j: (l, 0, j)),
                  pl.BlockSpec((1, 1, tn), lambda l, j: (l, 0, j))],
        out_specs=pl.BlockSpec((1, rows, tn), lambda l, j: (l, 0, j)),
        out_shape=jax.ShapeDtypeStruct((depth, rows, n), F32),
        compiler_params=_cparams(2),
        name="adaln",
    )(v_rows, w_ada, b_ada.reshape(depth, 1, n))


def _ffn_kernel(s_ref, mod_ref, g_ref, wg_ref, wu_ref, wd_ref, o_ref, act_scr, *, mod_i):
    x = s_ref[...]
    shift = mod_ref[0, mod_i:mod_i + 1, :]
    scale = mod_ref[0, mod_i + 1:mod_i + 2, :]
    gate = mod_ref[0, mod_i + 2:mod_i + 3, :]
    h = _norm_modulate(x, g_ref[...], shift, scale).astype(BF16)
    for j in range(D_FF // FF_CHUNK):
        cs = slice(j * FF_CHUNK, (j + 1) * FF_CHUNK)
        act_scr[:, cs] = (_silu(_dot(h, wg_ref[:, cs])) * _dot(h, wu_ref[:, cs])).astype(BF16)
    y = _dot(act_scr[...], wd_ref[...])
    o_ref[...] = x + 0.5 * gate * y


def _ffn(s, n_rows, mod, g_norm, wg, wu, wd, *, mod_i, seq, batch):
    tm = TM_FFN
    return pl.pallas_call(
        functools.partial(_ffn_kernel, mod_i=mod_i),
        grid=(n_rows // tm,),
        in_specs=[pl.BlockSpec((tm, D_MODEL), lambda i: (i, 0)),
                  pl.BlockSpec((1, N_MOD, D_MODEL), lambda i: (jnp.minimum(i * tm // seq, batch), 0, 0)),
                  pl.BlockSpec((1, D_MODEL), lambda i: (0, 0)),
                  _resident(), _resident(), _resident()],
        out_specs=pl.BlockSpec((tm, D_MODEL), lambda i: (i, 0)),
        out_shape=jax.ShapeDtypeStruct((n_rows, D_MODEL), F32),
        scratch_shapes=[pltpu.VMEM((tm, D_FF), BF16)],
        compiler_params=_cparams(1),
        name="ffn",
    )(s, mod, g_norm.reshape(1, D_MODEL), wg, wu, wd)


def _proj_kernel(s_ref, mod_ref, g_ref, w_ref, gmat_ref, qg_ref, kg_ref, cos_ref, sin_ref,
                 q_ref, k_ref, v_ref, z_ref, xbc_ref, gg_ref, dt_ref, h_scr):
    x = s_ref[...]
    h_scr[...] = _norm_modulate(x, g_ref[...], mod_ref[0, 3:4, :], mod_ref[0, 4:5, :]).astype(BF16)
    tm = x.shape[0]
    lane = lax.broadcasted_iota(jnp.int32, (tm, QK_CHUNK), 1)
    first_half = (lane % HEAD_DIM) < (HEAD_DIM // 2)
    cos = cos_ref[...]
    sin = sin_ref[...]

    def qk_epilogue(off, gain_ref, out_ref):
        for j in range(NA_WIDTH // QK_CHUNK):
            cs = slice(j * QK_CHUNK, (j + 1) * QK_CHUNK)
            y = _dot(h_scr[...], w_ref[:, off + j * QK_CHUNK:off + (j + 1) * QK_CHUNK])
            sq = y * y
            hi = sq.astype(BF16)
            lo = (sq - hi.astype(F32)).astype(BF16)
            ms = _dot(hi, gmat_ref[...]) + _dot(lo, gmat_ref[...])
            yn = y * lax.rsqrt(ms + EPS) * gain_ref[:, cs]
            swapped = jnp.where(first_half,
                                pltpu.roll(yn, QK_CHUNK - HEAD_DIM // 2, 1),
                                pltpu.roll(yn, HEAD_DIM // 2, 1))
            out_ref[:, cs] = (yn * cos + swapped * sin).astype(out_ref.dtype)

    qk_epilogue(OFF_Q, qg_ref, q_ref)
    qk_epilogue(OFF_K, kg_ref, k_ref)

    def plain(off, width, out_ref):
        for j in range(width // PROJ_CHUNK):
            y = _dot(h_scr[...], w_ref[:, off + j * PROJ_CHUNK:off + (j + 1) * PROJ_CHUNK])
            out_ref[:, j * PROJ_CHUNK:(j + 1) * PROJ_CHUNK] = y.astype(out_ref.dtype)

    plain(OFF_V, NA_WIDTH, v_ref)
    plain(OFF_Z, SSD_D_INNER, z_ref)
    plain(OFF_XBC, SSD_CONV_DIM, xbc_ref)
    plain(OFF_G, 2 * D_MODEL, gg_ref)
    dt_ref[...] = _dot(h_scr[...], w_ref[:, OFF_DT:OFF_DT + DT_PAD])


def _proj(s, mod, g_norm, w_packed, gmat, q_gain, k_gain, cos_tab, sin_tab, *, seq, batch):
    n_rows = s.shape[0]
    tm = TM_PROJ
    n_lat = batch * seq // tm
    tiles_per_seq = seq // tm

    def rope_idx(i):
        return (jnp.where(i < n_lat, i % tiles_per_seq, tiles_per_seq), 0)

    def rows(width):
        return pl.BlockSpec((tm, width), lambda i: (i, 0))

    def out(width, dtype):
        return jax.ShapeDtypeStruct((n_rows, width), dtype)

    return pl.pallas_call(
        _proj_kernel,
        grid=(n_rows // tm,),
        in_specs=[rows(D_MODEL),
                  pl.BlockSpec((1, N_MOD, D_MODEL), lambda i: (jnp.minimum(i * tm // seq, batch), 0, 0)),
                  pl.BlockSpec((1, D_MODEL), lambda i: (0, 0)),
                  _resident(), _resident(),
                  pl.BlockSpec((1, NA_WIDTH), lambda i: (0, 0)),
                  pl.BlockSpec((1, NA_WIDTH), lambda i: (0, 0)),
                  pl.BlockSpec((tm, QK_CHUNK), rope_idx),
                  pl.BlockSpec((tm, QK_CHUNK), rope_idx)],
        out_specs=[rows(NA_WIDTH), rows(NA_WIDTH), rows(NA_WIDTH), rows(SSD_D_INNER),
                   rows(SSD_CONV_DIM), rows(2 * D_MODEL), rows(DT_PAD)],
        out_shape=[out(NA_WIDTH, BF16), out(NA_WIDTH, BF16), out(NA_WIDTH, BF16), out(SSD_D_INNER, F32),
                   out(SSD_CONV_DIM, F32), out(2 * D_MODEL, F32), out(DT_PAD, F32)],
        scratch_shapes=[pltpu.VMEM((tm, D_MODEL), BF16)],
        compiler_params=_cparams(1),
        name="in_proj",
    )(s, mod, g_norm.reshape(1, D_MODEL), w_packed, gmat, q_gain, k_gain, cos_tab, sin_tab)


ATTN_STACK = HEAD_PAIR * GRID_W


def _stacked_queries(q_ref, n_grid_rows):
    low = lax.broadcasted_iota(jnp.int32, (GRID_W, V7X_LANES), 1) < HEAD_DIM
    parts = []
    for rr in range(n_grid_rows):
        qrow = q_ref[rr * GRID_W:(rr + 1) * GRID_W, :]
        zero = jnp.zeros_like(qrow)
        parts += [jnp.where(low, qrow, zero), jnp.where(low, zero, qrow)]
    return jnp.concatenate(parts, axis=0)


def _unstack_heads(o, denom, o_ref, n_grid_rows):
    low = lax.broadcasted_iota(jnp.int32, (GRID_W, V7X_LANES), 1) < HEAD_DIM
    o = o / denom
    rows = [jnp.where(low, o[rr * ATTN_STACK:rr * ATTN_STACK + GRID_W],
                      o[rr * ATTN_STACK + GRID_W:(rr + 1) * ATTN_STACK]) for rr in range(n_grid_rows)]
    o_ref[...] = jnp.concatenate(rows, axis=0).astype(o_ref.dtype)


def _attn_kernel(q_ref, k_ref, v_ref, kc_ref, vc_ref, bias_ref, o_ref, *, rows_per_step, grid_rows):
    rg = pl.program_id(2)
    win_keys = WIN_R * GRID_W
    qs = _stacked_queries(q_ref, rows_per_step)
    sc = _dot_nt(qs, kc_ref[...])
    sws, vws = [], []
    for rr in range(rows_per_step):
        r = rg * rows_per_step + rr
        r0 = jnp.clip(r - WIN_R // 2, 0, grid_rows - WIN_R)
        start = pl.multiple_of(r0 * GRID_W, GRID_W)
        d0 = (WIN_R - 1) - (r - r0)
        bias = jnp.concatenate([bias_ref[0, d0 + i] for i in range(0, WIN_R, HEAD_PAIR)], axis=1)
        sws.append(_dot_nt(qs[rr * ATTN_STACK:(rr + 1) * ATTN_STACK], k_ref[pl.ds(start, win_keys), :]) + bias)
        vws.append(v_ref[pl.ds(start, win_keys), :])
    sw = jnp.concatenate(sws, axis=0)
    m = jnp.maximum(jnp.max(sc, axis=-1, keepdims=True), jnp.max(sw, axis=-1, keepdims=True))
    pw = jnp.exp(sw - m)
    pc = jnp.exp(sc - m)
    denom = jnp.sum(pc, axis=-1, keepdims=True) + jnp.sum(pw, axis=-1, keepdims=True)
    pw = pw.astype(BF16)
    ow = jnp.concatenate([_dot(pw[rr * ATTN_STACK:(rr + 1) * ATTN_STACK], vws[rr])
                          for rr in range(rows_per_step)], axis=0)
    _unstack_heads(ow + _dot(pc.astype(BF16), vc_ref[...]), denom, o_ref, rows_per_step)


def _attention(q, k, v, bias, *, seq, ctx_len, batch):
    rows_per_step = ATTN_ROWS_PER_STEP
    tq = rows_per_step * GRID_W
    grid_rows = seq // GRID_W
    n_rg = grid_rows // rows_per_step
    ctx_blk0 = batch * seq // ctx_len
    return pl.pallas_call(
        functools.partial(_attn_kernel, rows_per_step=rows_per_step, grid_rows=grid_rows),
        grid=(N_PAIRS, batch, n_rg),
        in_specs=[pl.BlockSpec((tq, V7X_LANES), lambda j, b, rg: (b * n_rg + rg, j)),
                  pl.BlockSpec((seq, V7X_LANES), lambda j, b, rg: (b, j)),
                  pl.BlockSpec((seq, V7X_LANES), lambda j, b, rg: (b, j)),
                  pl.BlockSpec((ctx_len, V7X_LANES), lambda j, b, rg: (ctx_blk0 + b, j)),
                  pl.BlockSpec((ctx_len, V7X_LANES), lambda j, b, rg: (ctx_blk0 + b, j)),
                  pl.BlockSpec((1, 2 * WIN_R - 2, ATTN_STACK, 2 * GRID_W), lambda j, b, rg: (j, 0, 0, 0))],
        out_specs=pl.BlockSpec((tq, V7X_LANES), lambda j, b, rg: (b * n_rg + rg, j)),
        out_shape=jax.ShapeDtypeStruct((batch * seq, NA_WIDTH), BF16),
        compiler_params=_cparams(3),
        name="attention",
    )(q, k, v, k, v, bias)


def _ctx_attn_kernel(q_ref, kc_ref, vc_ref, o_ref, *, n_grid_rows):
    qs = _stacked_queries(q_ref, n_grid_rows)
    sc = _dot_nt(qs, kc_ref[...])
    pc = jnp.exp(sc - jnp.max(sc, axis=-1, keepdims=True))
    _unstack_heads(_dot(pc.astype(BF16), vc_ref[...]), jnp.sum(pc, axis=-1, keepdims=True), o_ref, n_grid_rows)


def _ctx_attention(q, k, v, *, seq, ctx_len, batch):
    ctx_blk0 = batch * seq // ctx_len
    spec = pl.BlockSpec((ctx_len, V7X_LANES), lambda j, b: (ctx_blk0 + b, j))
    return pl.pallas_call(
        functools.partial(_ctx_attn_kernel, n_grid_rows=ctx_len // GRID_W),
        grid=(N_PAIRS, batch),
        in_specs=[spec, spec, spec],
        out_specs=pl.BlockSpec((ctx_len, V7X_LANES), lambda j, b: (b, j)),
        out_shape=jax.ShapeDtypeStruct((batch * ctx_len, NA_WIDTH), BF16),
        compiler_params=_cparams(2),
        name="ctx_attention",
    )(q, k, v)


def _attention_bias(rpb):
    n_dr, n_dc = 2 * WIN_R - 1, 2 * WIN_C - 1
    period = 2 * GRID_W
    vec = jnp.full((NA_HEADS, n_dr, period), MASK_BIAS, F32)
    vec = vec.at[:, :, GRID_W - WIN_C:GRID_W - WIN_C + n_dc].set(rpb.astype(F32))
    tiled = jnp.tile(vec, (1, 1, GRID_W))[:, :, :GRID_W * (period - 1)]
    toe = tiled.reshape(NA_HEADS, n_dr, GRID_W, period - 1)[..., GRID_W - 1:]
    c = np.arange(GRID_W)[:, None]
    kc = np.arange(GRID_W)[None, :]
    s_col = np.clip(c - WIN_C // 2, 0, GRID_W - WIN_C)
    valid = (kc >= s_col) & (kc < s_col + WIN_C)
    toe = jnp.where(valid[None, None], toe, MASK_BIAS)
    two = jnp.concatenate([toe[:, :-1], toe[:, 1:]], axis=-1)
    two = two.reshape(N_PAIRS, HEAD_PAIR, n_dr - 1, GRID_W, period)
    return jnp.transpose(two, (0, 2, 1, 3, 4)).reshape(N_PAIRS, n_dr - 1, HEAD_PAIR * GRID_W, period)


def _scan_rows(x, reverse):
    n = x.shape[0]
    row = lax.broadcasted_iota(jnp.int32, x.shape, 0)
    step = 1
    while step < n:
        if reverse:
            x = x + jnp.where(row < n - step, pltpu.roll(x, n - step, 0), 0.0)
        else:
            x = x + jnp.where(row >= step, pltpu.roll(x, step, 0), 0.0)
        step *= 2
    return x


def _ssd_chunk(u_ref, dt_ref, dtb_ref, alog_ref, st_scr, write_y, *, reverse):
    q = SSD_CHUNK
    direction = 1 if reverse else 0
    dt = _softplus(dt_ref[...] + dtb_ref[...])
    a = -jnp.exp(alog_ref[...])
    ac = _scan_rows(dt * a, reverse)
    edge = 0 if reverse else q - 1
    e_all = jnp.exp(ac)
    wdt_t = (jnp.exp(ac[edge:edge + 1, :] - ac) * dt).T
    ac2 = ac * LOG2_E
    lg2_t = (ac2 - jnp.log2(dt)).T
    ii = lax.broadcasted_iota(jnp.int32, (q, q), 0)
    jj = lax.broadcasted_iota(jnp.int32, (q, q), 1)
    causal = (ii <= jj) if reverse else (ii >= jj)
    low = jj < SSD_HEADDIM
    for g in range(SSD_GROUPS):
        b_g = u_ref[:, SSD_D_INNER + g * SSD_STATE:SSD_D_INNER + (g + 1) * SSD_STATE]
        c_g = u_ref[:, SSD_D_INNER + SSD_BC + g * SSD_STATE:SSD_D_INNER + SSD_BC + (g + 1) * SSD_STATE]
        cb = _dot_nt(c_g.astype(BF16), b_g.astype(BF16))
        b_t = b_g.T
        for pp in range(PAIRS_PER_GROUP):
            pair = g * PAIRS_PER_GROUP + pp
            xp = u_ref[:, pair * V7X_LANES:(pair + 1) * V7X_LANES].astype(BF16)
            st = st_scr[pair]
            rhs = jnp.concatenate([xp, st.astype(BF16)], axis=0)
            zero = jnp.zeros_like(xp)
            x_split = jnp.concatenate([jnp.where(low, xp, zero), jnp.where(low, zero, xp)], axis=0)
            ys, bws, decays = [], [], []
            for half in range(HEAD_PAIR):
                hl = direction * SSD_HEADS + pair * HEAD_PAIR + half
                a_col = jnp.broadcast_to(ac2[:, hl:hl + 1], (q, q))
                e_col = jnp.broadcast_to(e_all[:, hl:hl + 1], (q, q))
                m = jnp.exp2(jnp.where(causal, a_col - lg2_t[hl:hl + 1, :], -jnp.inf)) * cb
                lhs = jnp.concatenate([m, c_g * e_col], axis=1).astype(BF16)
                ys.append(_dot(lhs, rhs))
                bws.append((b_t * wdt_t[hl:hl + 1, :]).astype(BF16))
                decays.append(e_col[edge:edge + 1, :])
            write_y(pair, jnp.where(low, ys[0], ys[1]))
            new = _dot(jnp.concatenate(bws, axis=1), x_split)
            st_scr[pair] = st * jnp.where(low[0:1], decays[0], decays[1]) + new


def _ssd_position(step, b, *, n_cc, n_lc, batch, reverse):
    is_ctx = step < n_cc
    loc = jnp.where(is_ctx, step, step - n_cc)
    if reverse:
        loc = jnp.where(is_ctx, n_cc - 1 - loc, n_lc - 1 - loc)
    blk = jnp.where(is_ctx, batch * n_lc + b * n_cc + loc, b * n_lc + loc)
    n_here = jnp.where(is_ctx, n_cc, n_lc)
    return blk, loc, n_here


def _ssd_fwd_kernel(xbc_ref, prev_ref, next_ref, dt_ref, cw_ref, cbias_ref, dtb_ref, alog_ref,
                    u_ref, y_ref, xs_scr, st_scr, *, n_cc, n_lc, batch):
    step = pl.program_id(1)

    @pl.when(step == 0)
    def _():
        st_scr[...] = jnp.zeros_like(st_scr)

    _, loc, n_here = _ssd_position(step, pl.program_id(0), n_cc=n_cc, n_lc=n_lc, batch=batch, reverse=False)
    halo = V7X_SUBLANES
    q = SSD_CHUNK
    for j in range(SSD_CONV_DIM // V7X_LANES):
        cs = slice(j * V7X_LANES, (j + 1) * V7X_LANES)
        xs_scr[j, 0:halo, :] = jnp.where(loc == 0, 0.0, prev_ref[:, cs])
        xs_scr[j, halo:halo + q, :] = xbc_ref[:, cs]
        xs_scr[j, halo + q:2 * halo + q, :] = jnp.where(loc == n_here - 1, 0.0, next_ref[:, cs])
        acc = jnp.broadcast_to(cbias_ref[:, cs], (q, V7X_LANES))
        for t in range(SSD_CONV):
            r0 = halo - SSD_CONV // 2 + t
            acc = acc + xs_scr[j, r0:r0 + q, :] * cw_ref[t:t + 1, cs]
        u_ref[:, cs] = _silu(acc)

    def write_y(pair, val):
        y_ref[:, pair * V7X_LANES:(pair + 1) * V7X_LANES] = val

    _ssd_chunk(u_ref, dt_ref, dtb_ref, alog_ref, st_scr, write_y, reverse=False)


def _ssd_bwd_kernel(u_ref, dt_ref, z_ref, yf_ref, dtb_ref, alog_ref, dskip_ref, nw_ref,
                    so_ref, y_scr, st_scr):
    step = pl.program_id(1)

    @pl.when(step == 0)
    def _():
        st_scr[...] = jnp.zeros_like(st_scr)

    def write_y(pair, val):
        y_scr[:, pair * V7X_LANES:(pair + 1) * V7X_LANES] = val

    _ssd_chunk(u_ref, dt_ref, dtb_ref, alog_ref, st_scr, write_y, reverse=True)
    gw = SSD_D_INNER // SSD_GROUPS
    for g in range(SSD_GROUPS):
        cs = slice(g * gw, (g + 1) * gw)
        y = yf_ref[:, cs] + y_scr[:, cs] + dskip_ref[:, cs] * u_ref[:, cs]
        y = y * _silu(z_ref[:, cs])
        ms = jnp.mean(y * y, axis=-1, keepdims=True)
        so_ref[:, cs] = (y * lax.rsqrt(ms + EPS) * nw_ref[:, cs]).astype(so_ref.dtype)


def _ssd(xbc, dt, z, conv_w, conv_b, dt_bias, a_log, d_skip, norm_w, *, seq, ctx_len, batch):
    n_rows = xbc.shape[0]
    q = SSD_CHUNK
    n_cc, n_lc = ctx_len // q, seq // q
    steps = n_cc + n_lc
    halo = V7X_SUBLANES
    halo_per_chunk = q // halo
    n_halo_blocks = n_rows // halo

    def pos(reverse):
        return functools.partial(_ssd_position, n_cc=n_cc, n_lc=n_lc, batch=batch, reverse=reverse)

    def chunk_rows(width, reverse):
        return pl.BlockSpec((q, width), lambda b, s: (pos(reverse)(s, b)[0], 0))

    def small(rows, width):
        return pl.BlockSpec((rows, width), lambda b, s: (0, 0))

    prev_spec = pl.BlockSpec(
        (halo, SSD_CONV_DIM), lambda b, s: (jnp.maximum(pos(False)(s, b)[0] * halo_per_chunk - 1, 0), 0))
    next_spec = pl.BlockSpec(
        (halo, SSD_CONV_DIM),
        lambda b, s: (jnp.minimum((pos(False)(s, b)[0] + 1) * halo_per_chunk, n_halo_blocks - 1), 0))

    conv_w_rows = jnp.zeros((V7X_SUBLANES, SSD_CONV_DIM), F32).at[:SSD_CONV].set(conv_w)
    pad = DT_PAD - 2 * SSD_HEADS
    dtb_row = jnp.pad(dt_bias.reshape(1, 2 * SSD_HEADS), ((0, 0), (0, pad)))
    alog_row = jnp.pad(a_log.reshape(1, 2 * SSD_HEADS), ((0, 0), (0, pad)))
    state = pltpu.VMEM((SSD_PAIRS, SSD_STATE, V7X_LANES), F32)

    u, y_f = pl.pallas_call(
        functools.partial(_ssd_fwd_kernel, n_cc=n_cc, n_lc=n_lc, batch=batch),
        grid=(batch, steps),
        in_specs=[chunk_rows(SSD_CONV_DIM, False), prev_spec, next_spec, chunk_rows(DT_PAD, False),
                  small(V7X_SUBLANES, SSD_CONV_DIM), small(1, SSD_CONV_DIM), small(1, DT_PAD), small(1, DT_PAD)],
        out_specs=[chunk_rows(SSD_CONV_DIM, False), chunk_rows(SSD_D_INNER, False)],
        out_shape=[jax.ShapeDtypeStruct((n_rows, SSD_CONV_DIM), F32),
                   jax.ShapeDtypeStruct((n_rows, SSD_D_INNER), F32)],
        scratch_shapes=[pltpu.VMEM((SSD_CONV_DIM // V7X_LANES, q + 2 * halo, V7X_LANES), F32), state],
        compiler_params=_cparams(2),
        name="ssd_fwd",
    )(xbc, xbc, xbc, dt, conv_w_rows, conv_b.reshape(1, SSD_CONV_DIM), dtb_row, alog_row)

    d_row = jnp.repeat(d_skip, SSD_HEADDIM).reshape(1, SSD_D_INNER)
    return pl.pallas_call(
        _ssd_bwd_kernel,
        grid=(batch, steps),
        in_specs=[chunk_rows(SSD_CONV_DIM, True), chunk_rows(DT_PAD, True), chunk_rows(SSD_D_INNER, True),
                  chunk_rows(SSD_D_INNER, True), small(1, DT_PAD), small(1, DT_PAD),
                  small(1, SSD_D_INNER), small(1, SSD_D_INNER)],
        out_specs=chunk_rows(SSD_D_INNER, True),
        out_shape=jax.ShapeDtypeStruct((n_rows, SSD_D_INNER), BF16),
        scratch_shapes=[pltpu.VMEM((q, SSD_D_INNER), F32), state],
        compiler_params=_cparams(2),
        name="ssd_bwd",
    )(u, dt, z, y_f, dtb_row, alog_row, d_row, norm_w.reshape(1, SSD_D_INNER))


def _merge_kernel(*refs, n_lat_tiles, with_ctx):
    if with_ctx:
        s_ref, mod_ref, a_ref, ac_ref, so_ref, g_ref, wa_ref, ws_ref, wo_ref, o_ref = refs
        a = jnp.where(pl.program_id(0) < n_lat_tiles, a_ref[...], ac_ref[...])
    else:
        s_ref, mod_ref, a_ref, so_ref, g_ref, wa_ref, ws_ref, wo_ref, o_ref = refs
        a = a_ref[...]
    ya = _dot(a, wa_ref[...])
    ys = _dot(so_ref[...], ws_ref[...])
    g = _sigmoid(g_ref[...].astype(F32))
    merged = (g[:, :D_MODEL] * ya + g[:, D_MODEL:] * ys).astype(BF16)
    o_ref[...] = s_ref[...] + mod_ref[0, 5:6, :] * _dot(merged, wo_ref[...])


def _merge(s, n_rows, mod, a_lat, a_ctx, s_o, g, wa, ws, wo, *, seq, batch):
    tm = TM_MERGE
    n_lat_tiles = batch * seq // tm
    with_ctx = a_ctx is not None

    def rows(width):
        return pl.BlockSpec((tm, width), lambda i: (i, 0))

    attn_specs = [pl.BlockSpec((tm, NA_WIDTH), lambda i: (jnp.minimum(i, n_lat_tiles - 1), 0))]
    attn_args = [a_lat]
    if with_ctx:
        attn_specs.append(pl.BlockSpec((tm, NA_WIDTH), lambda i: (jnp.maximum(i - n_lat_tiles, 0), 0)))
        attn_args.append(a_ctx)
    return pl.pallas_call(
        functools.partial(_merge_kernel, n_lat_tiles=n_lat_tiles, with_ctx=with_ctx),
        grid=(n_rows // tm,),
        in_specs=[rows(D_MODEL),
                  pl.BlockSpec((1, N_MOD, D_MODEL), lambda i: (jnp.minimum(i * tm // seq, batch), 0, 0)),
                  *attn_specs, rows(SSD_D_INNER), rows(2 * D_MODEL),
                  _resident(), _resident(), _resident()],
        out_specs=rows(D_MODEL),
        out_shape=jax.ShapeDtypeStruct((n_rows, D_MODEL), F32),
        compiler_params=_cparams(1),
        name="merge",
    )(s, mod, *attn_args, s_o, g, wa, ws, wo)


def _rope_tables(seq, pad_rows):
    t = np.arange(seq)
    row = (t // GRID_W).astype(np.float32)
    col = (t % GRID_W).astype(np.float32)
    n_freq = HEAD_DIM // 4
    inv = jnp.asarray(ROPE_BASE, F32) ** (-jnp.arange(n_freq, dtype=F32) / n_freq)
    ang = jnp.concatenate([jnp.asarray(row)[:, None] * inv, jnp.asarray(col)[:, None] * inv], axis=-1)
    cos, sin = jnp.cos(ang), jnp.sin(ang)
    reps = QK_CHUNK // HEAD_DIM
    cos_t = jnp.tile(jnp.concatenate([cos, cos], axis=-1), (1, reps))
    sin_t = jnp.tile(jnp.concatenate([-sin, sin], axis=-1), (1, reps))
    cos_t = jnp.concatenate([cos_t, jnp.ones((pad_rows, QK_CHUNK), F32)], axis=0)
    sin_t = jnp.concatenate([sin_t, jnp.zeros((pad_rows, QK_CHUNK), F32)], axis=0)
    return cos_t, sin_t


def _pack_w_in(w_in_l):
    sizes = (NA_WIDTH, NA_WIDTH, NA_WIDTH, SSD_D_INNER, SSD_CONV_DIM, 2 * SSD_HEADS, 2 * D_MODEL)
    offs = np.concatenate([[0], np.cumsum(sizes)])
    q, k, v, z, xbc, dt, g = (w_in_l[:, offs[i]:offs[i + 1]] for i in range(7))
    pad = jnp.zeros((D_MODEL, DT_PAD - 2 * SSD_HEADS), w_in_l.dtype)
    return jnp.concatenate([q, k, v, z, xbc, g, dt, pad], axis=1).astype(BF16)


def kernel(x, c, ctx, c_ctx, w_ada, b_ada, norm_ffn1, ffn1_w_gate, ffn1_w_up, ffn1_w_down, norm_mix, w_in,
           q_norm, k_norm, na_rpb, na_w_o, ssd_conv_w, ssd_conv_b, ssd_dt_bias, ssd_a_log, ssd_d, ssd_norm,
           ssd_w_o, w_out, norm_ffn2, ffn2_w_gate, ffn2_w_up, ffn2_w_down):
    batch, seq, d = x.shape
    ctx_len = ctx.shape[1]
    depth = w_ada.shape[0]
    n_lat = batch * seq
    n_all = n_lat + batch * ctx_len
    assert d == D_MODEL and seq % GRID_W == 0 and seq // GRID_W >= WIN_R
    assert seq % TM_FFN == 0 and (batch * ctx_len) % TM_FFN == 0 and seq % TM_PROJ == 0
    assert ctx_len % SSD_CHUNK == 0 and seq % ctx_len == 0 and ctx_len % GRID_W == 0
    assert (seq // GRID_W) % ATTN_ROWS_PER_STEP == 0 and (batch * ctx_len) % TM_MERGE == 0

    s = jnp.concatenate([x.reshape(n_lat, d), ctx.reshape(batch * ctx_len, d)], axis=0)

    mod_rows = -(-(batch + 1) // V7X_SUBLANES) * V7X_SUBLANES
    v_rows = jnp.zeros((mod_rows, d), F32).at[:batch].set(c).at[batch].set(c_ctx)
    mod_all = _adaln(v_rows, w_ada, b_ada).reshape(depth, mod_rows, N_MOD, d)

    cos_tab, sin_tab = _rope_tables(seq, TM_PROJ)
    gmat = jnp.asarray(np.kron(np.eye(QK_CHUNK // HEAD_DIM), np.full((HEAD_DIM, HEAD_DIM), 1.0 / HEAD_DIM)), BF16)

    for l in range(depth):
        need_ctx = l < depth - 1
        mod = mod_all[l]
        s = _ffn(s, n_all, mod, norm_ffn1[l], ffn1_w_gate[l].astype(BF16), ffn1_w_up[l].astype(BF16),
                 ffn1_w_down[l].astype(BF16), mod_i=0, seq=seq, batch=batch)
        q_gain = jnp.tile(q_norm[l] * ATTN_SCALE, NA_HEADS).reshape(1, NA_WIDTH)
        k_gain = jnp.tile(k_norm[l], NA_HEADS).reshape(1, NA_WIDTH)
        q, k, v, z, xbc, g, dt = _proj(s, mod, norm_mix[l], _pack_w_in(w_in[l]), gmat, q_gain, k_gain,
                                       cos_tab, sin_tab, seq=seq, batch=batch)
        a_lat = _attention(q, k, v, _attention_bias(na_rpb[l]), seq=seq, ctx_len=ctx_len, batch=batch)
        a_ctx = _ctx_attention(q, k, v, seq=seq, ctx_len=ctx_len, batch=batch) if need_ctx else None
        s_o = _ssd(xbc, dt, z, ssd_conv_w[l], ssd_conv_b[l], ssd_dt_bias[l], ssd_a_log[l], ssd_d[l],
                   ssd_norm[l], seq=seq, ctx_len=ctx_len, batch=batch)
        n_out = n_all if need_ctx else n_lat
        s = _merge(s, n_out, mod, a_lat, a_ctx, s_o, g, na_w_o[l].astype(BF16), ssd_w_o[l].astype(BF16),
                   w_out[l].astype(BF16), seq=seq, batch=batch)
        s = _ffn(s, n_out, mod, norm_ffn2[l], ffn2_w_gate[l].astype(BF16), ffn2_w_up[l].astype(BF16),
                 ffn2_w_down[l].astype(BF16), mod_i=6, seq=seq, batch=batch)
    return s[:n_lat].reshape(batch, seq, d)
```

```python
import functools

import numpy as np
import jax
import jax.numpy as jnp
from jax import lax
from jax.experimental import pallas as pl
from jax.experimental.pallas import tpu as pltpu

F32 = jnp.float32
BF16 = jnp.bfloat16

D_MODEL = 1024
GRID_W = 64
HEAD_DIM = 64
NA_HEADS = 16
NA_WIDTH = NA_HEADS * HEAD_DIM
WIN_R = 8
WIN_C = 16
ROPE_BASE = 10000.0
ATTN_SCALE = HEAD_DIM ** -0.5
SSD_D_INNER = 2 * D_MODEL
SSD_HEADDIM = 64
SSD_HEADS = SSD_D_INNER // SSD_HEADDIM
SSD_GROUPS = 4
SSD_STATE = 128
SSD_CONV = 5
SSD_CHUNK = 128
SSD_BC = SSD_GROUPS * SSD_STATE
SSD_CONV_DIM = SSD_D_INNER + 2 * SSD_BC
D_FF = 2816
N_MOD = 9
EPS = 1e-6

V7X_LANES = 128
V7X_SUBLANES = 8
V7X_MXU_DIM = 256
V7X_VMEM_LIMIT_BYTES = 56 * 1024 * 1024

HEAD_PAIR = V7X_LANES // HEAD_DIM
N_PAIRS = NA_HEADS // HEAD_PAIR
SSD_PAIRS = SSD_HEADS // HEAD_PAIR
PAIRS_PER_GROUP = SSD_PAIRS // SSD_GROUPS
DT_PAD = V7X_LANES
MASK_BIAS = -1e30
LOG2_E = 1.4426950408889634

TM_FFN = 1024
TM_PROJ = 512
TM_MERGE = 512
ATTN_ROWS_PER_STEP = 16
ATTN_GROUP_ROWS = 16
SSD_BLOCK = 2 * SSD_CHUNK
SSD_HALO = 2 * V7X_SUBLANES
FF_CHUNK = V7X_MXU_DIM
QK_CHUNK = V7X_MXU_DIM
PROJ_CHUNK = 512

OFF_Q = 0
OFF_K = OFF_Q + NA_WIDTH
OFF_V = OFF_K + NA_WIDTH
OFF_Z = OFF_V + NA_WIDTH
OFF_XBC = OFF_Z + SSD_D_INNER
OFF_G = OFF_XBC + SSD_CONV_DIM
OFF_DT = OFF_G + 2 * D_MODEL
N_IN_PACKED = OFF_DT + DT_PAD


def _cparams(n_axes):
    return pltpu.CompilerParams(dimension_semantics=("arbitrary",) * n_axes,
                                vmem_limit_bytes=V7X_VMEM_LIMIT_BYTES)


def _resident():
    return pl.BlockSpec(memory_space=pltpu.VMEM)


def _sigmoid(x):
    return 0.5 * jnp.tanh(0.5 * x) + 0.5


def _silu(x):
    h = 0.5 * x
    return h + h * jnp.tanh(h)


def _softplus(x):
    return jnp.maximum(x, 0.0) + jnp.log(1.0 + jnp.exp(-jnp.abs(x)))


def _dot(a, b):
    return jnp.dot(a, b, preferred_element_type=F32)


def _dot_nt(a, b):
    return lax.dot_general(a, b, (((1,), (1,)), ((), ())), preferred_element_type=F32)


def _norm_modulate(x, g_row, shift, scale):
    ms = jnp.mean(x * x, axis=-1, keepdims=True)
    y = x * lax.rsqrt(ms + EPS) * g_row
    return y * (1.0 + scale) + shift


def _adaln_kernel(v_ref, w_ref, b_ref, o_ref):
    sv = _silu(v_ref[...]).astype(BF16)
    o_ref[0] = _dot(sv, w_ref[0].astype(BF16)) + b_ref[0]


def _adaln(v_rows, w_ada, b_ada):
    depth, d, n = w_ada.shape
    rows = v_rows.shape[0]
    tn = D_MODEL
    return pl.pallas_call(
        _adaln_kernel,
        grid=(depth, n // tn),
        in_specs=[pl.BlockSpec((rows, d), lambda l, j: (0, 0)),
                  pl.BlockSpec((1, d, tn), lambda l, j: (l, 0, j)),
                  pl.BlockSpec((1, 1, tn), lambda l, j: (l, 0, j))],
        out_specs=pl.BlockSpec((1, rows, tn), lambda l, j: (l, 0, j)),
        out_shape=jax.ShapeDtypeStruct((depth, rows, n), F32),
        compiler_params=_cparams(2),
        name="adaln",
    )(v_rows, w_ada, b_ada.reshape(depth, 1, n))


def _ffn_kernel(s_ref, mod_ref, g_ref, wg_ref, wu_ref, wd_ref, o_ref, act_scr, *, mod_i):
    x = s_ref[...]
    shift = mod_ref[0, mod_i:mod_i + 1, :]
    scale = mod_ref[0, mod_i + 1:mod_i + 2, :]
    gate = mod_ref[0, mod_i + 2:mod_i + 3, :]
    h = _norm_modulate(x, g_ref[...], shift, scale).astype(BF16)
    for j in range(D_FF // FF_CHUNK):
        cs = slice(j * FF_CHUNK, (j + 1) * FF_CHUNK)
        act_scr[:, cs] = (_silu(_dot(h, wg_ref[:, cs])) * _dot(h, wu_ref[:, cs])).astype(BF16)
    y = _dot(act_scr[...], wd_ref[...])
    o_ref[...] = x + 0.5 * gate * y


def _ffn(s, n_rows, mod, g_norm, wg, wu, wd, *, mod_i, seq, batch):
    tm = TM_FFN
    return pl.pallas_call(
        functools.partial(_ffn_kernel, mod_i=mod_i),
        grid=(n_rows // tm,),
        in_specs=[pl.BlockSpec((tm, D_MODEL), lambda i: (i, 0)),
                  pl.BlockSpec((1, N_MOD, D_MODEL), lambda i: (jnp.minimum(i * tm // seq, batch), 0, 0)),
                  pl.BlockSpec((1, D_MODEL), lambda i: (0, 0)),
                  _resident(), _resident(), _resident()],
        out_specs=pl.BlockSpec((tm, D_MODEL), lambda i: (i, 0)),
        out_shape=jax.ShapeDtypeStruct((n_rows, D_MODEL), F32),
        scratch_shapes=[pltpu.VMEM((tm, D_FF), BF16)],
        compiler_params=_cparams(1),
        name="ffn",
    )(s, mod, g_norm.reshape(1, D_MODEL), wg, wu, wd)


def _proj_kernel(s_ref, mod_ref, g_ref, w_ref, gmat_ref, qg_ref, kg_ref, cos_ref, sin_ref,
                 q_ref, k_ref, v_ref, z_ref, xbc_ref, gg_ref, dt_ref, h_scr):
    x = s_ref[...]
    h_scr[...] = _norm_modulate(x, g_ref[...], mod_ref[0, 3:4, :], mod_ref[0, 4:5, :]).astype(BF16)
    tm = x.shape[0]
    lane = lax.broadcasted_iota(jnp.int32, (tm, QK_CHUNK), 1)
    first_half = (lane % HEAD_DIM) < (HEAD_DIM // 2)
    cos = cos_ref[...]
    sin = sin_ref[...]

    def qk_raw(off, j):
        y = _dot(h_scr[...], w_ref[:, off + j * QK_CHUNK:off + (j + 1) * QK_CHUNK])
        sq = y * y
        hi = sq.astype(BF16)
        lo = (sq - hi.astype(F32)).astype(BF16)
        return y, jnp.concatenate([hi, lo], axis=1)

    def qk_finish(gain_ref, out_ref, j, y, sq_split):
        cs = slice(j * QK_CHUNK, (j + 1) * QK_CHUNK)
        ms = _dot(sq_split, gmat_ref[...])
        yn = y * lax.rsqrt(ms + EPS) * gain_ref[:, cs]
        swapped = jnp.where(first_half,
                            pltpu.roll(yn, QK_CHUNK - HEAD_DIM // 2, 1),
                            pltpu.roll(yn, HEAD_DIM // 2, 1))
        out_ref[:, cs] = (yn * cos + swapped * sin).astype(out_ref.dtype)

    def plain_chunk(off, out_ref, j):
        y = _dot(h_scr[...], w_ref[:, off + j * PROJ_CHUNK:off + (j + 1) * PROJ_CHUNK])
        out_ref[:, j * PROJ_CHUNK:(j + 1) * PROJ_CHUNK] = y.astype(out_ref.dtype)

    qk_chunks = [(off, gain, out, j) for off, gain, out in ((OFF_Q, qg_ref, q_ref), (OFF_K, kg_ref, k_ref))
                 for j in range(NA_WIDTH // QK_CHUNK)]
    plain_tasks = [functools.partial(plain_chunk, off, out, j)
                   for off, width, out in ((OFF_V, NA_WIDTH, v_ref), (OFF_Z, SSD_D_INNER, z_ref),
                                           (OFF_XBC, SSD_CONV_DIM, xbc_ref), (OFF_G, 2 * D_MODEL, gg_ref))
                   for j in range(width // PROJ_CHUNK)]
    assert len(plain_tasks) == 2 * len(qk_chunks)
    for i, (off, gain, out, j) in enumerate(qk_chunks):
        y, sq_split = qk_raw(off, j)
        plain_tasks[2 * i]()
        qk_finish(gain, out, j, y, sq_split)
        plain_tasks[2 * i + 1]()
    dt_ref[...] = _dot(h_scr[...], w_ref[:, OFF_DT:OFF_DT + DT_PAD])


def _proj(s, mod, g_norm, w_packed, gmat, q_gain, k_gain, cos_tab, sin_tab, *, seq, batch):
    n_rows = s.shape[0]
    tm = TM_PROJ
    n_lat = batch * seq // tm
    tiles_per_seq = seq // tm

    def rope_idx(i):
        return (jnp.where(i < n_lat, i % tiles_per_seq, tiles_per_seq), 0)

    def rows(width):
        return pl.BlockSpec((tm, width), lambda i: (i, 0))

    def out(width, dtype):
        return jax.ShapeDtypeStruct((n_rows, width), dtype)

    return pl.pallas_call(
        _proj_kernel,
        grid=(n_rows // tm,),
        in_specs=[rows(D_MODEL),
                  pl.BlockSpec((1, N_MOD, D_MODEL), lambda i: (jnp.minimum(i * tm // seq, batch), 0, 0)),
                  pl.BlockSpec((1, D_MODEL), lambda i: (0, 0)),
                  _resident(), _resident(),
                  pl.BlockSpec((1, NA_WIDTH), lambda i: (0, 0)),
                  pl.BlockSpec((1, NA_WIDTH), lambda i: (0, 0)),
                  pl.BlockSpec((tm, QK_CHUNK), rope_idx),
                  pl.BlockSpec((tm, QK_CHUNK), rope_idx)],
        out_specs=[rows(NA_WIDTH), rows(NA_WIDTH), rows(NA_WIDTH), rows(SSD_D_INNER),
                   rows(SSD_CONV_DIM), rows(2 * D_MODEL), rows(DT_PAD)],
        out_shape=[out(NA_WIDTH, BF16), out(NA_WIDTH, BF16), out(NA_WIDTH, BF16), out(SSD_D_INNER, BF16),
                   out(SSD_CONV_DIM, BF16), out(2 * D_MODEL, BF16), out(DT_PAD, F32)],
        scratch_shapes=[pltpu.VMEM((tm, D_MODEL), BF16)],
        compiler_params=_cparams(1),
        name="in_proj",
    )(s, mod, g_norm.reshape(1, D_MODEL), w_packed, gmat, q_gain, k_gain, cos_tab, sin_tab)


ATTN_STACK = HEAD_PAIR * GRID_W


def _stacked_queries(q_ref, n_grid_rows):
    low = lax.broadcasted_iota(jnp.int32, (GRID_W, V7X_LANES), 1) < HEAD_DIM
    parts = []
    for rr in range(n_grid_rows):
        qrow = q_ref[rr * GRID_W:(rr + 1) * GRID_W, :]
        zero = jnp.zeros_like(qrow)
        parts += [jnp.where(low, qrow, zero), jnp.where(low, zero, qrow)]
    return jnp.concatenate(parts, axis=0)


def _unstack_heads(o, denom, o_ref, n_grid_rows):
    low = lax.broadcasted_iota(jnp.int32, (GRID_W, V7X_LANES), 1) < HEAD_DIM
    o = o / denom
    rows = [jnp.where(low, o[rr * ATTN_STACK:rr * ATTN_STACK + GRID_W],
                      o[rr * ATTN_STACK + GRID_W:(rr + 1) * ATTN_STACK]) for rr in range(n_grid_rows)]
    o_ref[...] = jnp.concatenate(rows, axis=0).astype(o_ref.dtype)


def _attn_kernel(q_ref, k_ref, v_ref, kc_ref, vc_ref, bias_ref, o_ref, *, rows_per_step, grid_rows):
    rg = pl.program_id(2)
    win_keys = WIN_R * GRID_W
    qs = _stacked_queries(q_ref, rows_per_step)

    def window_start(rr):
        r = rg * rows_per_step + rr
        r0 = jnp.clip(r - WIN_R // 2, 0, grid_rows - WIN_R)
        return r, r0, pl.multiple_of(r0 * GRID_W, GRID_W)

    gs = ATTN_GROUP_ROWS

    def scores(g):
        sws = []
        for rr in range(g * gs, (g + 1) * gs):
            r, r0, start = window_start(rr)
            d0 = (WIN_R - 1) - (r - r0)
            bias = jnp.concatenate([bias_ref[0, d0 + i] for i in range(0, WIN_R, HEAD_PAIR)], axis=1)
            sws.append(_dot_nt(qs[rr * ATTN_STACK:(rr + 1) * ATTN_STACK], k_ref[pl.ds(start, win_keys), :]) + bias)
        sc = _dot_nt(qs[g * gs * ATTN_STACK:(g + 1) * gs * ATTN_STACK], kc_ref[...])
        return jnp.concatenate(sws, axis=0), sc

    def probs(sw, sc):
        m = jnp.maximum(jnp.max(sc, axis=-1, keepdims=True), jnp.max(sw, axis=-1, keepdims=True))
        return jnp.exp2((sw - m).astype(BF16)), jnp.exp2((sc - m).astype(BF16))

    def weighted(g, pw, pc):
        ones_w = jnp.ones((win_keys, V7X_LANES), BF16)
        ones_c = jnp.ones((kc_ref.shape[0], V7X_LANES), BF16)
        ows = []
        for i, rr in enumerate(range(g * gs, (g + 1) * gs)):
            _, _, start = window_start(rr)
            v_aug = jnp.concatenate([v_ref[pl.ds(start, win_keys), :], ones_w], axis=1)
            ows.append(_dot(pw[i * ATTN_STACK:(i + 1) * ATTN_STACK], v_aug))
        return jnp.concatenate(ows, axis=0) + _dot(pc, jnp.concatenate([vc_ref[...], ones_c], axis=1))

    outs = []
    for g in range(rows_per_step // gs):
        outs.append(weighted(g, *probs(*scores(g))))
    o_aug = jnp.concatenate(outs, axis=0)
    _unstack_heads(o_aug[:, :V7X_LANES], o_aug[:, V7X_LANES:], o_ref, rows_per_step)


def _attention(q, k, v, bias, *, seq, ctx_len, batch):
    rows_per_step = ATTN_ROWS_PER_STEP
    tq = rows_per_step * GRID_W
    grid_rows = seq // GRID_W
    n_rg = grid_rows // rows_per_step
    ctx_blk0 = batch * seq // ctx_len
    return pl.pallas_call(
        functools.partial(_attn_kernel, rows_per_step=rows_per_step, grid_rows=grid_rows),
        grid=(N_PAIRS, batch, n_rg),
        in_specs=[pl.BlockSpec((tq, V7X_LANES), lambda j, b, rg: (b * n_rg + rg, j)),
                  pl.BlockSpec((seq, V7X_LANES), lambda j, b, rg: (b, j)),
                  pl.BlockSpec((seq, V7X_LANES), lambda j, b, rg: (b, j)),
                  pl.BlockSpec((ctx_len, V7X_LANES), lambda j, b, rg: (ctx_blk0 + b, j)),
                  pl.BlockSpec((ctx_len, V7X_LANES), lambda j, b, rg: (ctx_blk0 + b, j)),
                  pl.BlockSpec((1, 2 * WIN_R - 2, ATTN_STACK, 2 * GRID_W), lambda j, b, rg: (j, 0, 0, 0))],
        out_specs=pl.BlockSpec((tq, V7X_LANES), lambda j, b, rg: (b * n_rg + rg, j)),
        out_shape=jax.ShapeDtypeStruct((batch * seq, NA_WIDTH), BF16),
        compiler_params=_cparams(3),
        name="attention",
    )(q, k, v, k, v, bias)


def _ctx_attn_kernel(q_ref, kc_ref, vc_ref, o_ref, *, n_grid_rows):
    qs = _stacked_queries(q_ref, n_grid_rows)
    sc = _dot_nt(qs, kc_ref[...])
    pc = jnp.exp2(sc - jnp.max(sc, axis=-1, keepdims=True))
    _unstack_heads(_dot(pc.astype(BF16), vc_ref[...]), jnp.sum(pc, axis=-1, keepdims=True), o_ref, n_grid_rows)


def _ctx_attention(q, k, v, *, seq, ctx_len, batch):
    ctx_blk0 = batch * seq // ctx_len
    spec = pl.BlockSpec((ctx_len, V7X_LANES), lambda j, b: (ctx_blk0 + b, j))
    return pl.pallas_call(
        functools.partial(_ctx_attn_kernel, n_grid_rows=ctx_len // GRID_W),
        grid=(N_PAIRS, batch),
        in_specs=[spec, spec, spec],
        out_specs=pl.BlockSpec((ctx_len, V7X_LANES), lambda j, b: (b, j)),
        out_shape=jax.ShapeDtypeStruct((batch * ctx_len, NA_WIDTH), BF16),
        compiler_params=_cparams(2),
        name="ctx_attention",
    )(q, k, v)


def _attention_bias(rpb):
    n_dr, n_dc = 2 * WIN_R - 1, 2 * WIN_C - 1
    period = 2 * GRID_W
    vec = jnp.full((NA_HEADS, n_dr, period), MASK_BIAS, F32)
    vec = vec.at[:, :, GRID_W - WIN_C:GRID_W - WIN_C + n_dc].set(rpb.astype(F32) * LOG2_E)
    tiled = jnp.tile(vec, (1, 1, GRID_W))[:, :, :GRID_W * (period - 1)]
    toe = tiled.reshape(NA_HEADS, n_dr, GRID_W, period - 1)[..., GRID_W - 1:]
    c = np.arange(GRID_W)[:, None]
    kc = np.arange(GRID_W)[None, :]
    s_col = np.clip(c - WIN_C // 2, 0, GRID_W - WIN_C)
    valid = (kc >= s_col) & (kc < s_col + WIN_C)
    toe = jnp.where(valid[None, None], toe, MASK_BIAS)
    two = jnp.concatenate([toe[:, :-1], toe[:, 1:]], axis=-1)
    two = two.reshape(N_PAIRS, HEAD_PAIR, n_dr - 1, GRID_W, period)
    return jnp.transpose(two, (0, 2, 1, 3, 4)).reshape(N_PAIRS, n_dr - 1, HEAD_PAIR * GRID_W, period)


def _scan_rows(x, reverse):
    n = x.shape[0]
    row = lax.broadcasted_iota(jnp.int32, x.shape, 0)
    step = 1
    while step < n:
        if reverse:
            x = x + jnp.where(row < n - step, pltpu.roll(x, n - step, 0), 0.0)
        else:
            x = x + jnp.where(row >= step, pltpu.roll(x, step, 0), 0.0)
        step *= 2
    return x


def _ssd_chunk(get_x, get_b, get_c, dt_raw, dtb_ref, alog_ref, st_scr, write_y, *, reverse):
    q = SSD_CHUNK
    direction = 1 if reverse else 0
    dt = _softplus(dt_raw + dtb_ref[...])
    a = -jnp.exp(alog_ref[...])
    ac = _scan_rows(dt * a, reverse)
    edge = 0 if reverse else q - 1
    e_all = jnp.exp(ac)
    wdt_t = (jnp.exp(ac[edge:edge + 1, :] - ac) * dt).T
    ac2 = ac * LOG2_E
    lg2_t = (ac2 - jnp.log2(dt)).T
    ii = lax.broadcasted_iota(jnp.int32, (q, q), 0)
    jj = lax.broadcasted_iota(jnp.int32, (q, q), 1)
    causal = (ii <= jj) if reverse else (ii >= jj)
    low = jj < SSD_HEADDIM
    for g in range(SSD_GROUPS):
        b_g = get_b(g)
        c_g = get_c(g)
        cb = _dot_nt(c_g.astype(BF16), b_g.astype(BF16))
        b_t = b_g.T
        for pp in range(PAIRS_PER_GROUP):
            pair = g * PAIRS_PER_GROUP + pp
            xp = get_x(pair)
            st = st_scr[pair]
            rhs = jnp.concatenate([xp, st.astype(BF16)], axis=0)
            zero = jnp.zeros_like(xp)
            x_split = jnp.concatenate([jnp.where(low, xp, zero), jnp.where(low, zero, xp)], axis=0)
            ys, bws, decays = [], [], []
            for half in range(HEAD_PAIR):
                hl = direction * SSD_HEADS + pair * HEAD_PAIR + half
                a_col = jnp.broadcast_to(ac2[:, hl:hl + 1], (q, q))
                e_col = jnp.broadcast_to(e_all[:, hl:hl + 1], (q, q))
                m = jnp.exp2(jnp.where(causal, a_col - lg2_t[hl:hl + 1, :], -jnp.inf)) * cb
                lhs = jnp.concatenate([m, c_g * e_col], axis=1).astype(BF16)
                ys.append(_dot(lhs, rhs))
                bws.append((b_t * wdt_t[hl:hl + 1, :]).astype(BF16))
                decays.append(e_col[edge:edge + 1, :])
            write_y(pair, jnp.where(low, ys[0], ys[1]))
            new = _dot(jnp.concatenate(bws, axis=1), x_split)
            st_scr[pair] = st * jnp.where(low[0:1], decays[0], decays[1]) + new


def _ssd_position(step, b, *, n_cb, n_lb, batch, reverse):
    is_ctx = step < n_cb
    loc = jnp.where(is_ctx, step, step - n_cb)
    if reverse:
        loc = jnp.where(is_ctx, n_cb - 1 - loc, n_lb - 1 - loc)
    blk = jnp.where(is_ctx, batch * n_lb + b * n_cb + loc, b * n_lb + loc)
    n_here = jnp.where(is_ctx, n_cb, n_lb)
    return blk, loc, n_here


def _ssd_fwd_kernel(xbc_ref, prev_ref, next_ref, dt_ref, cw_ref, cbias_ref, dtb_ref, alog_ref,
                    u_ref, y_ref, xs_scr, bc_scr, st_scr, *, n_cb, n_lb, batch):
    step = pl.program_id(1)

    @pl.when(step == 0)
    def _():
        st_scr[...] = jnp.zeros_like(st_scr)

    _, loc, n_here = _ssd_position(step, pl.program_id(0), n_cb=n_cb, n_lb=n_lb, batch=batch, reverse=False)
    halo = SSD_HALO
    rows = SSD_BLOCK
    at_start = loc == 0
    at_end = loc == n_here - 1
    for j in range(SSD_CONV_DIM // V7X_LANES):
        cs = slice(j * V7X_LANES, (j + 1) * V7X_LANES)
        xs_scr[j, 0:halo, :] = jnp.where(at_start, 0.0, prev_ref[:, cs].astype(F32))
        xs_scr[j, halo:halo + rows, :] = xbc_ref[:, cs].astype(F32)
        xs_scr[j, halo + rows:2 * halo + rows, :] = jnp.where(at_end, 0.0, next_ref[:, cs].astype(F32))
        acc = jnp.broadcast_to(cbias_ref[:, cs], (rows, V7X_LANES))
        for t in range(SSD_CONV):
            r0 = halo - SSD_CONV // 2 + t
            acc = acc + xs_scr[j, r0:r0 + rows, :] * cw_ref[t:t + 1, cs]
        u = _silu(acc)
        u_ref[:, cs] = u.astype(u_ref.dtype)
        if j * V7X_LANES >= SSD_D_INNER:
            bc_scr[:, j * V7X_LANES - SSD_D_INNER:(j + 1) * V7X_LANES - SSD_D_INNER] = u

    for ci in range(SSD_BLOCK // SSD_CHUNK):
        rs = slice(ci * SSD_CHUNK, (ci + 1) * SSD_CHUNK)

        def write_y(pair, val, rs=rs):
            y_ref[rs, pair * V7X_LANES:(pair + 1) * V7X_LANES] = val.astype(y_ref.dtype)

        _ssd_chunk(lambda pair, rs=rs: u_ref[rs, pair * V7X_LANES:(pair + 1) * V7X_LANES],
                   lambda g, rs=rs: bc_scr[rs, g * SSD_STATE:(g + 1) * SSD_STATE],
                   lambda g, rs=rs: bc_scr[rs, SSD_BC + g * SSD_STATE:SSD_BC + (g + 1) * SSD_STATE],
                   dt_ref[rs, :], dtb_ref, alog_ref, st_scr, write_y, reverse=False)


def _ssd_bwd_kernel(u_ref, dt_ref, z_ref, yf_ref, dtb_ref, alog_ref, dskip_ref, nw_ref,
                    so_ref, y_scr, st_scr):
    step = pl.program_id(1)

    @pl.when(step == 0)
    def _():
        st_scr[...] = jnp.zeros_like(st_scr)

    gw = SSD_D_INNER // SSD_GROUPS
    for ci in reversed(range(SSD_BLOCK // SSD_CHUNK)):
        rs = slice(ci * SSD_CHUNK, (ci + 1) * SSD_CHUNK)

        def write_y(pair, val):
            y_scr[:, pair * V7X_LANES:(pair + 1) * V7X_LANES] = val

        def get_bc(g, off, rs=rs):
            return u_ref[rs, SSD_D_INNER + off + g * SSD_STATE:SSD_D_INNER + off + (g + 1) * SSD_STATE].astype(F32)

        _ssd_chunk(lambda pair, rs=rs: u_ref[rs, pair * V7X_LANES:(pair + 1) * V7X_LANES],
                   lambda g: get_bc(g, 0), lambda g: get_bc(g, SSD_BC),
                   dt_ref[rs, :], dtb_ref, alog_ref, st_scr, write_y, reverse=True)
        for g in range(SSD_GROUPS):
            cs = slice(g * gw, (g + 1) * gw)
            y = yf_ref[rs, cs].astype(F32) + y_scr[:, cs] + dskip_ref[:, cs] * u_ref[rs, cs].astype(F32)
            y = y * _silu(z_ref[rs, cs].astype(F32))
            ms = jnp.mean(y * y, axis=-1, keepdims=True)
            so_ref[rs, cs] = (y * lax.rsqrt(ms + EPS) * nw_ref[:, cs]).astype(so_ref.dtype)


def _ssd(xbc, dt, z, conv_w, conv_b, dt_bias, a_log, d_skip, norm_w, *, seq, ctx_len, batch):
    n_rows = xbc.shape[0]
    rows = SSD_BLOCK
    n_cb, n_lb = ctx_len // rows, seq // rows
    steps = n_cb + n_lb
    halo = SSD_HALO
    halo_per_block = rows // halo
    n_halo_blocks = n_rows // halo

    def pos(reverse):
        return functools.partial(_ssd_position, n_cb=n_cb, n_lb=n_lb, batch=batch, reverse=reverse)

    def block_rows(width, reverse):
        return pl.BlockSpec((rows, width), lambda b, s: (pos(reverse)(s, b)[0], 0))

    def small(n, width):
        return pl.BlockSpec((n, width), lambda b, s: (0, 0))

    prev_spec = pl.BlockSpec(
        (halo, SSD_CONV_DIM), lambda b, s: (jnp.maximum(pos(False)(s, b)[0] * halo_per_block - 1, 0), 0))
    next_spec = pl.BlockSpec(
        (halo, SSD_CONV_DIM),
        lambda b, s: (jnp.minimum((pos(False)(s, b)[0] + 1) * halo_per_block, n_halo_blocks - 1), 0))

    conv_w_rows = jnp.zeros((V7X_SUBLANES, SSD_CONV_DIM), F32).at[:SSD_CONV].set(conv_w)
    pad = DT_PAD - 2 * SSD_HEADS
    dtb_row = jnp.pad(dt_bias.reshape(1, 2 * SSD_HEADS), ((0, 0), (0, pad)))
    alog_row = jnp.pad(a_log.reshape(1, 2 * SSD_HEADS), ((0, 0), (0, pad)))
    state = pltpu.VMEM((SSD_PAIRS, SSD_STATE, V7X_LANES), F32)

    u, y_f = pl.pallas_call(
        functools.partial(_ssd_fwd_kernel, n_cb=n_cb, n_lb=n_lb, batch=batch),
        grid=(batch, steps),
        in_specs=[block_rows(SSD_CONV_DIM, False), prev_spec, next_spec, block_rows(DT_PAD, False),
                  small(V7X_SUBLANES, SSD_CONV_DIM), small(1, SSD_CONV_DIM), small(1, DT_PAD), small(1, DT_PAD)],
        out_specs=[block_rows(SSD_CONV_DIM, False), block_rows(SSD_D_INNER, False)],
        out_shape=[jax.ShapeDtypeStruct((n_rows, SSD_CONV_DIM), BF16),
                   jax.ShapeDtypeStruct((n_rows, SSD_D_INNER), BF16)],
        scratch_shapes=[pltpu.VMEM((SSD_CONV_DIM // V7X_LANES, rows + 2 * halo, V7X_LANES), F32),
                        pltpu.VMEM((rows, 2 * SSD_BC), F32), state],
        compiler_params=_cparams(2),
        name="ssd_fwd",
    )(xbc, xbc, xbc, dt, conv_w_rows, conv_b.reshape(1, SSD_CONV_DIM), dtb_row, alog_row)

    d_row = jnp.repeat(d_skip, SSD_HEADDIM).reshape(1, SSD_D_INNER)
    return pl.pallas_call(
        _ssd_bwd_kernel,
        grid=(batch, steps),
        in_specs=[block_rows(SSD_CONV_DIM, True), block_rows(DT_PAD, True), block_rows(SSD_D_INNER, True),
                  block_rows(SSD_D_INNER, True), small(1, DT_PAD), small(1, DT_PAD),
                  small(1, SSD_D_INNER), small(1, SSD_D_INNER)],
        out_specs=block_rows(SSD_D_INNER, True),
        out_shape=jax.ShapeDtypeStruct((n_rows, SSD_D_INNER), BF16),
        scratch_shapes=[pltpu.VMEM((SSD_CHUNK, SSD_D_INNER), F32), state],
        compiler_params=_cparams(2),
        name="ssd_bwd",
    )(u, dt, z, y_f, dtb_row, alog_row, d_row, norm_w.reshape(1, SSD_D_INNER))


def _merge_kernel(*refs, n_lat_tiles, with_ctx):
    if with_ctx:
        s_ref, mod_ref, a_ref, ac_ref, so_ref, g_ref, wa_ref, ws_ref, wo_ref, o_ref = refs
        a = jnp.where(pl.program_id(0) < n_lat_tiles, a_ref[...], ac_ref[...])
    else:
        s_ref, mod_ref, a_ref, so_ref, g_ref, wa_ref, ws_ref, wo_ref, o_ref = refs
        a = a_ref[...]
    ya = _dot(a, wa_ref[...])
    ys = _dot(so_ref[...], ws_ref[...])
    g = _sigmoid(g_ref[...].astype(F32))
    merged = (g[:, :D_MODEL] * ya + g[:, D_MODEL:] * ys).astype(BF16)
    o_ref[...] = s_ref[...] + mod_ref[0, 5:6, :] * _dot(merged, wo_ref[...])


def _merge(s, n_rows, mod, a_lat, a_ctx, s_o, g, wa, ws, wo, *, seq, batch):
    tm = TM_MERGE
    n_lat_tiles = batch * seq // tm
    with_ctx = a_ctx is not None

    def rows(width):
        return pl.BlockSpec((tm, width), lambda i: (i, 0))

    attn_specs = [pl.BlockSpec((tm, NA_WIDTH), lambda i: (jnp.minimum(i, n_lat_tiles - 1), 0))]
    attn_args = [a_lat]
    if with_ctx:
        attn_specs.append(pl.BlockSpec((tm, NA_WIDTH), lambda i: (jnp.maximum(i - n_lat_tiles, 0), 0)))
        attn_args.append(a_ctx)
    return pl.pallas_call(
        functools.partial(_merge_kernel, n_lat_tiles=n_lat_tiles, with_ctx=with_ctx),
        grid=(n_rows // tm,),
        in_specs=[rows(D_MODEL),
                  pl.BlockSpec((1, N_MOD, D_MODEL), lambda i: (jnp.minimum(i * tm // seq, batch), 0, 0)),
                  *attn_specs, rows(SSD_D_INNER), rows(2 * D_MODEL),
                  _resident(), _resident(), _resident()],
        out_specs=rows(D_MODEL),
        out_shape=jax.ShapeDtypeStruct((n_rows, D_MODEL), F32),
        compiler_params=_cparams(1),
        name="merge",
    )(s, mod, *attn_args, s_o, g, wa, ws, wo)


def _rope_tables(seq, pad_rows):
    t = np.arange(seq)
    row = (t // GRID_W).astype(np.float32)
    col = (t % GRID_W).astype(np.float32)
    n_freq = HEAD_DIM // 4
    inv = jnp.asarray(ROPE_BASE, F32) ** (-jnp.arange(n_freq, dtype=F32) / n_freq)
    ang = jnp.concatenate([jnp.asarray(row)[:, None] * inv, jnp.asarray(col)[:, None] * inv], axis=-1)
    cos, sin = jnp.cos(ang), jnp.sin(ang)
    reps = QK_CHUNK // HEAD_DIM
    cos_t = jnp.tile(jnp.concatenate([cos, cos], axis=-1), (1, reps))
    sin_t = jnp.tile(jnp.concatenate([-sin, sin], axis=-1), (1, reps))
    cos_t = jnp.concatenate([cos_t, jnp.ones((pad_rows, QK_CHUNK), F32)], axis=0)
    sin_t = jnp.concatenate([sin_t, jnp.zeros((pad_rows, QK_CHUNK), F32)], axis=0)
    return cos_t, sin_t


def _pack_w_in(w_in_l):
    sizes = (NA_WIDTH, NA_WIDTH, NA_WIDTH, SSD_D_INNER, SSD_CONV_DIM, 2 * SSD_HEADS, 2 * D_MODEL)
    offs = np.concatenate([[0], np.cumsum(sizes)])
    q, k, v, z, xbc, dt, g = (w_in_l[:, offs[i]:offs[i + 1]] for i in range(7))
    pad = jnp.zeros((D_MODEL, DT_PAD - 2 * SSD_HEADS), w_in_l.dtype)
    return jnp.concatenate([q, k, v, z, xbc, g, dt, pad], axis=1).astype(BF16)


def kernel(x, c, ctx, c_ctx, w_ada, b_ada, norm_ffn1, ffn1_w_gate, ffn1_w_up, ffn1_w_down, norm_mix, w_in,
           q_norm, k_norm, na_rpb, na_w_o, ssd_conv_w, ssd_conv_b, ssd_dt_bias, ssd_a_log, ssd_d, ssd_norm,
           ssd_w_o, w_out, norm_ffn2, ffn2_w_gate, ffn2_w_up, ffn2_w_down):
    batch, seq, d = x.shape
    ctx_len = ctx.shape[1]
    depth = w_ada.shape[0]
    n_lat = batch * seq
    n_all = n_lat + batch * ctx_len
    assert d == D_MODEL and seq % GRID_W == 0 and seq // GRID_W >= WIN_R
    assert seq % TM_FFN == 0 and (batch * ctx_len) % TM_FFN == 0 and seq % TM_PROJ == 0
    assert ctx_len % SSD_BLOCK == 0 and seq % ctx_len == 0 and ctx_len % GRID_W == 0
    assert (seq // GRID_W) % ATTN_ROWS_PER_STEP == 0 and (batch * ctx_len) % TM_MERGE == 0

    s = jnp.concatenate([x.reshape(n_lat, d), ctx.reshape(batch * ctx_len, d)], axis=0)

    mod_rows = -(-(batch + 1) // V7X_SUBLANES) * V7X_SUBLANES
    v_rows = jnp.zeros((mod_rows, d), F32).at[:batch].set(c).at[batch].set(c_ctx)
    mod_all = _adaln(v_rows, w_ada, b_ada).reshape(depth, mod_rows, N_MOD, d)

    cos_tab, sin_tab = _rope_tables(seq, TM_PROJ)
    head_avg = np.kron(np.eye(QK_CHUNK // HEAD_DIM), np.full((HEAD_DIM, HEAD_DIM), 1.0 / HEAD_DIM))
    gmat = jnp.asarray(np.concatenate([head_avg, head_avg], axis=0), BF16)

    for l in range(depth):
        need_ctx = l < depth - 1
        mod = mod_all[l]
        s = _ffn(s, n_all, mod, norm_ffn1[l], ffn1_w_gate[l].astype(BF16), ffn1_w_up[l].astype(BF16),
                 ffn1_w_down[l].astype(BF16), mod_i=0, seq=seq, batch=batch)
        q_gain = jnp.tile(q_norm[l] * (ATTN_SCALE * LOG2_E), NA_HEADS).reshape(1, NA_WIDTH)
        k_gain = jnp.tile(k_norm[l], NA_HEADS).reshape(1, NA_WIDTH)
        q, k, v, z, xbc, g, dt = _proj(s, mod, norm_mix[l], _pack_w_in(w_in[l]), gmat, q_gain, k_gain,
                                       cos_tab, sin_tab, seq=seq, batch=batch)
        a_lat = _attention(q, k, v, _attention_bias(na_rpb[l]), seq=seq, ctx_len=ctx_len, batch=batch)
        a_ctx = _ctx_attention(q, k, v, seq=seq, ctx_len=ctx_len, batch=batch) if need_ctx else None
        s_o = _ssd(xbc, dt, z, ssd_conv_w[l], ssd_conv_b[l], ssd_dt_bias[l], ssd_a_log[l], ssd_d[l],
                   ssd_norm[l], seq=seq, ctx_len=ctx_len, batch=batch)
        n_out = n_all if need_ctx else n_lat
        s = _merge(s, n_out, mod, a_lat, a_ctx, s_o, g, na_w_o[l].astype(BF16), ssd_w_o[l].astype(BF16),
                   w_out[l].astype(BF16), seq=seq, batch=batch)
        s = _ffn(s, n_out, mod, norm_ffn2[l], ffn2_w_gate[l].astype(BF16), ffn2_w_up[l].astype(BF16),
                 ffn2_w_down[l].astype(BF16), mod_i=6, seq=seq, batch=batch)
    return s[:n_lat].reshape(batch, seq, d)
```

```python
import functools

import numpy as np
import jax
import jax.numpy as jnp
from jax import lax
from jax.experimental import pallas as pl
from jax.experimental.pallas import tpu as pltpu

F32 = jnp.float32
BF16 = jnp.bfloat16

D_MODEL = 1024
GRID_W = 64
HEAD_DIM = 64
NA_HEADS = 16
NA_WIDTH = NA_HEADS * HEAD_DIM
WIN_R = 8
WIN_C = 16
ROPE_BASE = 10000.0
ATTN_SCALE = HEAD_DIM ** -0.5
SSD_D_INNER = 2 * D_MODEL
SSD_HEADDIM = 64
SSD_HEADS = SSD_D_INNER // SSD_HEADDIM
SSD_GROUPS = 4
SSD_STATE = 128
SSD_CONV = 5
SSD_CHUNK = 128
SSD_BC = SSD_GROUPS * SSD_STATE
SSD_CONV_DIM = SSD_D_INNER + 2 * SSD_BC
D_FF = 2816
N_MOD = 9
EPS = 1e-6

V7X_LANES = 128
V7X_SUBLANES = 8
V7X_MXU_DIM = 256
V7X_VMEM_LIMIT_BYTES = 56 * 1024 * 1024

HEAD_PAIR = V7X_LANES // HEAD_DIM
N_PAIRS = NA_HEADS // HEAD_PAIR
SSD_PAIRS = SSD_HEADS // HEAD_PAIR
PAIRS_PER_GROUP = SSD_PAIRS // SSD_GROUPS
DT_PAD = V7X_LANES
MASK_BIAS = -1e30
LOG2_E = 1.4426950408889634

TM_FFN = 1024
TM_PROJ = 512
TM_MERGE = 512
ATTN_ROWS_PER_STEP = 16
ATTN_GROUP_ROWS = 16
SSD_BLOCK = 2 * SSD_CHUNK
PROJ_HALO = 2 * V7X_SUBLANES
CONV_SEG = 256
FF_CHUNK = V7X_MXU_DIM
QK_CHUNK = V7X_MXU_DIM
PROJ_CHUNK = 512

OFF_Q = 0
OFF_K = OFF_Q + NA_WIDTH
OFF_V = OFF_K + NA_WIDTH
OFF_Z = OFF_V + NA_WIDTH
OFF_XBC = OFF_Z + SSD_D_INNER
OFF_G = OFF_XBC + SSD_CONV_DIM
OFF_DT = OFF_G + 2 * D_MODEL
N_IN_PACKED = OFF_DT + DT_PAD


def _cparams(n_axes):
    return pltpu.CompilerParams(dimension_semantics=("arbitrary",) * n_axes,
                                vmem_limit_bytes=V7X_VMEM_LIMIT_BYTES)


def _resident():
    return pl.BlockSpec(memory_space=pltpu.VMEM)


def _sigmoid(x):
    return 0.5 * jnp.tanh(0.5 * x) + 0.5


def _silu(x):
    h = 0.5 * x
    return h + h * jnp.tanh(h)


def _softplus(x):
    return jnp.maximum(x, 0.0) + jnp.log(1.0 + jnp.exp(-jnp.abs(x)))


def _dot(a, b):
    return jnp.dot(a, b, preferred_element_type=F32)


def _dot_nt(a, b):
    return lax.dot_general(a, b, (((1,), (1,)), ((), ())), preferred_element_type=F32)


def _norm_modulate(x, g_row, shift, scale):
    ms = jnp.mean(x * x, axis=-1, keepdims=True)
    y = x * lax.rsqrt(ms + EPS) * g_row
    return y * (1.0 + scale) + shift


def _adaln_kernel(v_ref, w_ref, b_ref, o_ref):
    sv = _silu(v_ref[...]).astype(BF16)
    o_ref[0] = _dot(sv, w_ref[0].astype(BF16)) + b_ref[0]


def _adaln(v_rows, w_ada, b_ada):
    depth, d, n = w_ada.shape
    rows = v_rows.shape[0]
    tn = D_MODEL
    return pl.pallas_call(
        _adaln_kernel,
        grid=(depth, n // tn),
        in_specs=[pl.BlockSpec((rows, d), lambda l, j: (0, 0)),
                  pl.BlockSpec((1, d, tn), lambda l, j: (l, 0, j)),
                  pl.BlockSpec((1, 1, tn), lambda l, j: (l, 0, j))],
        out_specs=pl.BlockSpec((1, rows, tn), lambda l, j: (l, 0, j)),
        out_shape=jax.ShapeDtypeStruct((depth, rows, n), F32),
        compiler_params=_cparams(2),
        name="adaln",
    )(v_rows, w_ada, b_ada.reshape(depth, 1, n))


def _ffn_body(x, mod_ref, g_ref, wg_ref, wu_ref, wd_ref, act_scr, mod_i):
    shift = mod_ref[0, mod_i:mod_i + 1, :]
    scale = mod_ref[0, mod_i + 1:mod_i + 2, :]
    gate = mod_ref[0, mod_i + 2:mod_i + 3, :]
    h = _norm_modulate(x, g_ref[...], shift, scale).astype(BF16)
    for j in range(D_FF // FF_CHUNK):
        cs = slice(j * FF_CHUNK, (j + 1) * FF_CHUNK)
        act_scr[:, cs] = (_silu(_dot(h, wg_ref[:, cs])) * _dot(h, wu_ref[:, cs])).astype(BF16)
    y = _dot(act_scr[...], wd_ref[...])
    return x + 0.5 * gate * y


def _ffn_kernel(s_ref, mod_ref, g_ref, wg_ref, wu_ref, wd_ref, o_ref, act_scr, *, mod_i):
    o_ref[...] = _ffn_body(s_ref[...], mod_ref, g_ref, wg_ref, wu_ref, wd_ref, act_scr, mod_i)


def _ffn(s, n_rows, mod, g_norm, wg, wu, wd, *, mod_i, seq, batch):
    tm = TM_FFN
    return pl.pallas_call(
        functools.partial(_ffn_kernel, mod_i=mod_i),
        grid=(n_rows // tm,),
        in_specs=[pl.BlockSpec((tm, D_MODEL), lambda i: (i, 0)),
                  pl.BlockSpec((1, N_MOD, D_MODEL), lambda i: (jnp.minimum(i * tm // seq, batch), 0, 0)),
                  pl.BlockSpec((1, D_MODEL), lambda i: (0, 0)),
                  _resident(), _resident(), _resident()],
        out_specs=pl.BlockSpec((tm, D_MODEL), lambda i: (i, 0)),
        out_shape=jax.ShapeDtypeStruct((n_rows, D_MODEL), F32),
        scratch_shapes=[pltpu.VMEM((tm, D_FF), BF16)],
        compiler_params=_cparams(1),
        name="ffn",
    )(s, mod, g_norm.reshape(1, D_MODEL), wg, wu, wd)


def _proj_kernel(s_ref, sp_ref, sn_ref, mod_ref, g_ref, w_ref, gmat_ref, qg_ref, kg_ref, cos_ref, sin_ref,
                 cw_ref, cbias_ref, q_ref, k_ref, v_ref, z_ref, u_ref, gg_ref, dt_ref, h_scr, slab_scr,
                 *, n_lat_tiles, seq, ctx_len):
    tm = s_ref.shape[0]

    def hidden(x):
        return _norm_modulate(x, g_ref[...], mod_ref[0, 3:4, :], mod_ref[0, 4:5, :]).astype(BF16)

    h_scr[0:PROJ_HALO, :] = hidden(sp_ref[...])
    h_scr[PROJ_HALO:PROJ_HALO + tm, :] = hidden(s_ref[...])
    h_scr[PROJ_HALO + tm:, :] = hidden(sn_ref[...])

    def h_main():
        return h_scr[PROJ_HALO:PROJ_HALO + tm, :]

    lane = lax.broadcasted_iota(jnp.int32, (tm, QK_CHUNK), 1)
    first_half = (lane % HEAD_DIM) < (HEAD_DIM // 2)
    cos = cos_ref[...]
    sin = sin_ref[...]

    def qk_raw(off, j):
        y = _dot(h_main(), w_ref[:, off + j * QK_CHUNK:off + (j + 1) * QK_CHUNK])
        sq = y * y
        hi = sq.astype(BF16)
        lo = (sq - hi.astype(F32)).astype(BF16)
        return y, jnp.concatenate([hi, lo], axis=1)

    def qk_finish(gain_ref, out_ref, j, y, sq_split):
        cs = slice(j * QK_CHUNK, (j + 1) * QK_CHUNK)
        ms = _dot(sq_split, gmat_ref[...])
        yn = y * lax.rsqrt(ms + EPS) * gain_ref[:, cs]
        swapped = jnp.where(first_half,
                            pltpu.roll(yn, QK_CHUNK - HEAD_DIM // 2, 1),
                            pltpu.roll(yn, HEAD_DIM // 2, 1))
        out_ref[:, cs] = (yn * cos + swapped * sin).astype(out_ref.dtype)

    def plain_chunk(off, out_ref, j):
        y = _dot(h_main(), w_ref[:, off + j * PROJ_CHUNK:off + (j + 1) * PROJ_CHUNK])
        out_ref[:, j * PROJ_CHUNK:(j + 1) * PROJ_CHUNK] = y.astype(out_ref.dtype)

    tile = pl.program_id(0)
    is_ctx = tile >= n_lat_tiles
    row0 = jnp.where(is_ctx, tile - n_lat_tiles, tile) * tm
    seq_len = jnp.where(is_ctx, ctx_len, seq)
    seg_starts = [row0 + sg * CONV_SEG for sg in range(tm // CONV_SEG)]
    zero_prev = [lax.rem(st, seq_len) == 0 for st in seg_starts]
    zero_next = [lax.rem(st + CONV_SEG, seq_len) == 0 for st in seg_starts]

    def conv_chunk(j):
        y = _dot(h_scr[...], w_ref[:, OFF_XBC + j * PROJ_CHUNK:OFF_XBC + (j + 1) * PROJ_CHUNK])
        par = j % 2
        for lb in range(PROJ_CHUNK // V7X_LANES):
            cs = slice(j * PROJ_CHUNK + lb * V7X_LANES, j * PROJ_CHUNK + (lb + 1) * V7X_LANES)
            ylb = y[:, lb * V7X_LANES:(lb + 1) * V7X_LANES]
            for sg in range(tm // CONV_SEG):
                lo = sg * CONV_SEG
                slab_scr[par, lb, sg, 0:PROJ_HALO, :] = jnp.where(zero_prev[sg], 0.0, ylb[lo:lo + PROJ_HALO])
                slab_scr[par, lb, sg, PROJ_HALO:PROJ_HALO + CONV_SEG, :] = ylb[lo + PROJ_HALO:lo + PROJ_HALO + CONV_SEG]
                slab_scr[par, lb, sg, PROJ_HALO + CONV_SEG:, :] = jnp.where(
                    zero_next[sg], 0.0, ylb[lo + PROJ_HALO + CONV_SEG:lo + 2 * PROJ_HALO + CONV_SEG])
                acc = jnp.broadcast_to(cbias_ref[:, cs], (CONV_SEG, V7X_LANES))
                for t in range(SSD_CONV):
                    r0 = PROJ_HALO - SSD_CONV // 2 + t
                    acc = acc + slab_scr[par, lb, sg, r0:r0 + CONV_SEG, :] * cw_ref[t:t + 1, cs]
                u_ref[lo:lo + CONV_SEG, cs] = _silu(acc).astype(u_ref.dtype)

    qk_chunks = [(off, gain, out, j) for off, gain, out in ((OFF_Q, qg_ref, q_ref), (OFF_K, kg_ref, k_ref))
                 for j in range(NA_WIDTH // QK_CHUNK)]
    plain_tasks = [functools.partial(plain_chunk, off, out, j)
                   for off, width, out in ((OFF_V, NA_WIDTH, v_ref), (OFF_Z, SSD_D_INNER, z_ref),
                                           (OFF_G, 2 * D_MODEL, gg_ref))
                   for j in range(width // PROJ_CHUNK)]
    conv_tasks = [functools.partial(conv_chunk, j) for j in range(SSD_CONV_DIM // PROJ_CHUNK)]
    plain_tasks = [t for pair in zip(plain_tasks[:len(conv_tasks)], conv_tasks) for t in pair] \
        + plain_tasks[len(conv_tasks):]
    assert len(plain_tasks) == 2 * len(qk_chunks)
    for i, (off, gain, out, j) in enumerate(qk_chunks):
        y, sq_split = qk_raw(off, j)
        plain_tasks[2 * i]()
        qk_finish(gain, out, j, y, sq_split)
        plain_tasks[2 * i + 1]()
    dt_ref[...] = _dot(h_main(), w_ref[:, OFF_DT:OFF_DT + DT_PAD])


def _proj(s, mod, g_norm, w_packed, gmat, q_gain, k_gain, cos_tab, sin_tab, conv_w, conv_b, *, seq, ctx_len,
          batch):
    n_rows = s.shape[0]
    tm = TM_PROJ
    n_lat = batch * seq // tm
    tiles_per_seq = seq // tm
    halo_per_tile = tm // PROJ_HALO
    n_halo_blocks = n_rows // PROJ_HALO

    def rope_idx(i):
        return (jnp.where(i < n_lat, i % tiles_per_seq, tiles_per_seq), 0)

    def rows(width):
        return pl.BlockSpec((tm, width), lambda i: (i, 0))

    def const(n, width):
        return pl.BlockSpec((n, width), lambda i: (0, 0))

    def out(width, dtype):
        return jax.ShapeDtypeStruct((n_rows, width), dtype)

    conv_w_rows = jnp.zeros((V7X_SUBLANES, SSD_CONV_DIM), F32).at[:SSD_CONV].set(conv_w)
    slabs = pltpu.VMEM((2, PROJ_CHUNK // V7X_LANES, tm // CONV_SEG, CONV_SEG + 2 * PROJ_HALO, V7X_LANES), F32)
    return pl.pallas_call(
        functools.partial(_proj_kernel, n_lat_tiles=n_lat, seq=seq, ctx_len=ctx_len),
        grid=(n_rows // tm,),
        in_specs=[rows(D_MODEL),
                  pl.BlockSpec((PROJ_HALO, D_MODEL), lambda i: (jnp.maximum(i * halo_per_tile - 1, 0), 0)),
                  pl.BlockSpec((PROJ_HALO, D_MODEL),
                               lambda i: (jnp.minimum((i + 1) * halo_per_tile, n_halo_blocks - 1), 0)),
                  pl.BlockSpec((1, N_MOD, D_MODEL), lambda i: (jnp.minimum(i * tm // seq, batch), 0, 0)),
                  const(1, D_MODEL),
                  _resident(), _resident(),
                  const(1, NA_WIDTH), const(1, NA_WIDTH),
                  pl.BlockSpec((tm, QK_CHUNK), rope_idx),
                  pl.BlockSpec((tm, QK_CHUNK), rope_idx),
                  const(V7X_SUBLANES, SSD_CONV_DIM), const(1, SSD_CONV_DIM)],
        out_specs=[rows(NA_WIDTH), rows(NA_WIDTH), rows(NA_WIDTH), rows(SSD_D_INNER),
                   rows(SSD_CONV_DIM), rows(2 * D_MODEL), rows(DT_PAD)],
        out_shape=[out(NA_WIDTH, BF16), out(NA_WIDTH, BF16), out(NA_WIDTH, BF16), out(SSD_D_INNER, BF16),
                   out(SSD_CONV_DIM, BF16), out(2 * D_MODEL, BF16), out(DT_PAD, F32)],
        scratch_shapes=[pltpu.VMEM((tm + 2 * PROJ_HALO, D_MODEL), BF16), slabs],
        compiler_params=_cparams(1),
        name="in_proj",
    )(s, s, s, mod, g_norm.reshape(1, D_MODEL), w_packed, gmat, q_gain, k_gain, cos_tab, sin_tab,
      conv_w_rows, conv_b.reshape(1, SSD_CONV_DIM))


ATTN_STACK = HEAD_PAIR * GRID_W


def _stacked_queries(q_ref, n_grid_rows):
    low = lax.broadcasted_iota(jnp.int32, (GRID_W, V7X_LANES), 1) < HEAD_DIM
    parts = []
    for rr in range(n_grid_rows):
        qrow = q_ref[rr * GRID_W:(rr + 1) * GRID_W, :]
        zero = jnp.zeros_like(qrow)
        parts += [jnp.where(low, qrow, zero), jnp.where(low, zero, qrow)]
    return jnp.concatenate(parts, axis=0)


def _unstack_heads(o, denom, o_ref, n_grid_rows):
    low = lax.broadcasted_iota(jnp.int32, (GRID_W, V7X_LANES), 1) < HEAD_DIM
    o = o / denom
    rows = [jnp.where(low, o[rr * ATTN_STACK:rr * ATTN_STACK + GRID_W],
                      o[rr * ATTN_STACK + GRID_W:(rr + 1) * ATTN_STACK]) for rr in range(n_grid_rows)]
    o_ref[...] = jnp.concatenate(rows, axis=0).astype(o_ref.dtype)


def _attn_kernel(q_ref, k_ref, v_ref, kc_ref, vc_ref, bias_ref, o_ref, *, rows_per_step, grid_rows):
    rg = pl.program_id(2)
    win_keys = WIN_R * GRID_W
    qs = _stacked_queries(q_ref, rows_per_step)

    def window_start(rr):
        r = rg * rows_per_step + rr
        r0 = jnp.clip(r - WIN_R // 2, 0, grid_rows - WIN_R)
        return r, r0, pl.multiple_of(r0 * GRID_W, GRID_W)

    gs = ATTN_GROUP_ROWS

    def scores(g):
        sws = []
        for rr in range(g * gs, (g + 1) * gs):
            r, r0, start = window_start(rr)
            d0 = (WIN_R - 1) - (r - r0)
            bias = jnp.concatenate([bias_ref[0, d0 + i] for i in range(0, WIN_R, HEAD_PAIR)], axis=1)
            sws.append(_dot_nt(qs[rr * ATTN_STACK:(rr + 1) * ATTN_STACK], k_ref[pl.ds(start, win_keys), :]) + bias)
        sc = _dot_nt(qs[g * gs * ATTN_STACK:(g + 1) * gs * ATTN_STACK], kc_ref[...])
        return jnp.concatenate(sws, axis=0), sc

    def probs(sw, sc):
        m = jnp.maximum(jnp.max(sc, axis=-1, keepdims=True), jnp.max(sw, axis=-1, keepdims=True))
        return jnp.exp2((sw - m).astype(BF16)), jnp.exp2((sc - m).astype(BF16))

    def weighted(g, pw, pc):
        ones_w = jnp.ones((win_keys, V7X_LANES), BF16)
        ones_c = jnp.ones((kc_ref.shape[0], V7X_LANES), BF16)
        ows = []
        for i, rr in enumerate(range(g * gs, (g + 1) * gs)):
            _, _, start = window_start(rr)
            v_aug = jnp.concatenate([v_ref[pl.ds(start, win_keys), :], ones_w], axis=1)
            ows.append(_dot(pw[i * ATTN_STACK:(i + 1) * ATTN_STACK], v_aug))
        return jnp.concatenate(ows, axis=0) + _dot(pc, jnp.concatenate([vc_ref[...], ones_c], axis=1))

    outs = []
    for g in range(rows_per_step // gs):
        outs.append(weighted(g, *probs(*scores(g))))
    o_aug = jnp.concatenate(outs, axis=0)
    _unstack_heads(o_aug[:, :V7X_LANES], o_aug[:, V7X_LANES:], o_ref, rows_per_step)


def _attention(q, k, v, bias, *, seq, ctx_len, batch):
    rows_per_step = ATTN_ROWS_PER_STEP
    tq = rows_per_step * GRID_W
    grid_rows = seq // GRID_W
    n_rg = grid_rows // rows_per_step
    ctx_blk0 = batch * seq // ctx_len
    return pl.pallas_call(
        functools.partial(_attn_kernel, rows_per_step=rows_per_step, grid_rows=grid_rows),
        grid=(N_PAIRS, batch, n_rg),
        in_specs=[pl.BlockSpec((tq, V7X_LANES), lambda j, b, rg: (b * n_rg + rg, j)),
                  pl.BlockSpec((seq, V7X_LANES), lambda j, b, rg: (b, j)),
                  pl.BlockSpec((seq, V7X_LANES), lambda j, b, rg: (b, j)),
                  pl.BlockSpec((ctx_len, V7X_LANES), lambda j, b, rg: (ctx_blk0 + b, j)),
                  pl.BlockSpec((ctx_len, V7X_LANES), lambda j, b, rg: (ctx_blk0 + b, j)),
                  pl.BlockSpec((1, 2 * WIN_R - 2, ATTN_STACK, 2 * GRID_W), lambda j, b, rg: (j, 0, 0, 0))],
        out_specs=pl.BlockSpec((tq, V7X_LANES), lambda j, b, rg: (b * n_rg + rg, j)),
        out_shape=jax.ShapeDtypeStruct((batch * seq, NA_WIDTH), BF16),
        compiler_params=_cparams(3),
        name="attention",
    )(q, k, v, k, v, bias)


def _ctx_attn_kernel(q_ref, kc_ref, vc_ref, o_ref, *, n_grid_rows):
    qs = _stacked_queries(q_ref, n_grid_rows)
    sc = _dot_nt(qs, kc_ref[...])
    pc = jnp.exp2(sc - jnp.max(sc, axis=-1, keepdims=True))
    _unstack_heads(_dot(pc.astype(BF16), vc_ref[...]), jnp.sum(pc, axis=-1, keepdims=True), o_ref, n_grid_rows)


def _ctx_attention(q, k, v, *, seq, ctx_len, batch):
    ctx_blk0 = batch * seq // ctx_len
    spec = pl.BlockSpec((ctx_len, V7X_LANES), lambda j, b: (ctx_blk0 + b, j))
    return pl.pallas_call(
        functools.partial(_ctx_attn_kernel, n_grid_rows=ctx_len // GRID_W),
        grid=(N_PAIRS, batch),
        in_specs=[spec, spec, spec],
        out_specs=pl.BlockSpec((ctx_len, V7X_LANES), lambda j, b: (b, j)),
        out_shape=jax.ShapeDtypeStruct((batch * ctx_len, NA_WIDTH), BF16),
        compiler_params=_cparams(2),
        name="ctx_attention",
    )(q, k, v)


def _attention_bias(rpb):
    n_dr, n_dc = 2 * WIN_R - 1, 2 * WIN_C - 1
    period = 2 * GRID_W
    vec = jnp.full((NA_HEADS, n_dr, period), MASK_BIAS, F32)
    vec = vec.at[:, :, GRID_W - WIN_C:GRID_W - WIN_C + n_dc].set(rpb.astype(F32) * LOG2_E)
    tiled = jnp.tile(vec, (1, 1, GRID_W))[:, :, :GRID_W * (period - 1)]
    toe = tiled.reshape(NA_HEADS, n_dr, GRID_W, period - 1)[..., GRID_W - 1:]
    c = np.arange(GRID_W)[:, None]
    kc = np.arange(GRID_W)[None, :]
    s_col = np.clip(c - WIN_C // 2, 0, GRID_W - WIN_C)
    valid = (kc >= s_col) & (kc < s_col + WIN_C)
    toe = jnp.where(valid[None, None], toe, MASK_BIAS)
    two = jnp.concatenate([toe[:, :-1], toe[:, 1:]], axis=-1)
    two = two.reshape(N_PAIRS, HEAD_PAIR, n_dr - 1, GRID_W, period)
    return jnp.transpose(two, (0, 2, 1, 3, 4)).reshape(N_PAIRS, n_dr - 1, HEAD_PAIR * GRID_W, period)


def _scan_rows(x, reverse):
    n = x.shape[0]
    row = lax.broadcasted_iota(jnp.int32, x.shape, 0)
    step = 1
    while step < n:
        if reverse:
            x = x + jnp.where(row < n - step, pltpu.roll(x, n - step, 0), 0.0)
        else:
            x = x + jnp.where(row >= step, pltpu.roll(x, step, 0), 0.0)
        step *= 2
    return x


def _ssd_chunk(get_x, get_b, get_c, dt_raw, dtb_ref, alog_ref, st_scr, write_y, *, reverse):
    q = SSD_CHUNK
    direction = 1 if reverse else 0
    dt = _softplus(dt_raw + dtb_ref[...])
    a = -jnp.exp(alog_ref[...])
    ac = _scan_rows(dt * a, reverse)
    edge = 0 if reverse else q - 1
    e_all = jnp.exp(ac)
    wdt_t = (jnp.exp(ac[edge:edge + 1, :] - ac) * dt).T
    ac2 = ac * LOG2_E
    lg2_t = (ac2 - jnp.log2(dt)).T
    ii = lax.broadcasted_iota(jnp.int32, (q, q), 0)
    jj = lax.broadcasted_iota(jnp.int32, (q, q), 1)
    causal = (ii <= jj) if reverse else (ii >= jj)
    low = jj < SSD_HEADDIM
    for g in range(SSD_GROUPS):
        b_g = get_b(g)
        c_g = get_c(g)
        cb = _dot_nt(c_g.astype(BF16), b_g.astype(BF16))
        b_t = b_g.T
        for pp in range(PAIRS_PER_GROUP):
            pair = g * PAIRS_PER_GROUP + pp
            xp = get_x(pair)
            st = st_scr[pair]
            rhs = jnp.concatenate([xp, st.astype(BF16)], axis=0)
            zero = jnp.zeros_like(xp)
            x_split = jnp.concatenate([jnp.where(low, xp, zero), jnp.where(low, zero, xp)], axis=0)
            ys, bws, decays = [], [], []
            for half in range(HEAD_PAIR):
                hl = direction * SSD_HEADS + pair * HEAD_PAIR + half
                a_col = jnp.broadcast_to(ac2[:, hl:hl + 1], (q, q))
                e_col = jnp.broadcast_to(e_all[:, hl:hl + 1], (q, q))
                m = jnp.exp2(jnp.where(causal, a_col - lg2_t[hl:hl + 1, :], -jnp.inf)) * cb
                lhs = jnp.concatenate([m, c_g * e_col], axis=1).astype(BF16)
                ys.append(_dot(lhs, rhs))
                bws.append((b_t * wdt_t[hl:hl + 1, :]).astype(BF16))
                decays.append(e_col[edge:edge + 1, :])
            write_y(pair, jnp.where(low, ys[0], ys[1]))
            new = _dot(jnp.concatenate(bws, axis=1), x_split)
            st_scr[pair] = st * jnp.where(low[0:1], decays[0], decays[1]) + new


def _ssd_position(step, b, *, n_cb, n_lb, batch, reverse):
    is_ctx = step < n_cb
    loc = jnp.where(is_ctx, step, step - n_cb)
    if reverse:
        loc = jnp.where(is_ctx, n_cb - 1 - loc, n_lb - 1 - loc)
    blk = jnp.where(is_ctx, batch * n_lb + b * n_cb + loc, b * n_lb + loc)
    return blk, loc


def _ssd_chunk_from_u(u_ref, rs, dt_ref, dtb_ref, alog_ref, st_scr, write_y, *, reverse):
    def get_bc(g, off):
        return u_ref[rs, SSD_D_INNER + off + g * SSD_STATE:SSD_D_INNER + off + (g + 1) * SSD_STATE].astype(F32)

    _ssd_chunk(lambda pair: u_ref[rs, pair * V7X_LANES:(pair + 1) * V7X_LANES],
               lambda g: get_bc(g, 0), lambda g: get_bc(g, SSD_BC),
               dt_ref[rs, :], dtb_ref, alog_ref, st_scr, write_y, reverse=reverse)


def _ssd_fwd_kernel(u_ref, dt_ref, dtb_ref, alog_ref, y_ref, st_scr):
    step = pl.program_id(1)

    @pl.when(step == 0)
    def _():
        st_scr[...] = jnp.zeros_like(st_scr)

    for ci in range(SSD_BLOCK // SSD_CHUNK):
        rs = slice(ci * SSD_CHUNK, (ci + 1) * SSD_CHUNK)

        def write_y(pair, val, rs=rs):
            y_ref[rs, pair * V7X_LANES:(pair + 1) * V7X_LANES] = val.astype(y_ref.dtype)

        _ssd_chunk_from_u(u_ref, rs, dt_ref, dtb_ref, alog_ref, st_scr, write_y, reverse=False)


def _ssd_bwd_kernel(u_ref, dt_ref, z_ref, yf_ref, dtb_ref, alog_ref, dskip_ref, nw_ref,
                    so_ref, y_scr, st_scr):
    step = pl.program_id(1)

    @pl.when(step == 0)
    def _():
        st_scr[...] = jnp.zeros_like(st_scr)

    gw = SSD_D_INNER // SSD_GROUPS
    for ci in reversed(range(SSD_BLOCK // SSD_CHUNK)):
        rs = slice(ci * SSD_CHUNK, (ci + 1) * SSD_CHUNK)

        def write_y(pair, val):
            y_scr[:, pair * V7X_LANES:(pair + 1) * V7X_LANES] = val

        _ssd_chunk_from_u(u_ref, rs, dt_ref, dtb_ref, alog_ref, st_scr, write_y, reverse=True)
        for g in range(SSD_GROUPS):
            cs = slice(g * gw, (g + 1) * gw)
            y = yf_ref[rs, cs].astype(F32) + y_scr[:, cs] + dskip_ref[:, cs] * u_ref[rs, cs].astype(F32)
            y = y * _silu(z_ref[rs, cs].astype(F32))
            ms = jnp.mean(y * y, axis=-1, keepdims=True)
            so_ref[rs, cs] = (y * lax.rsqrt(ms + EPS) * nw_ref[:, cs]).astype(so_ref.dtype)


def _ssd(u, dt, z, dt_bias, a_log, d_skip, norm_w, *, seq, ctx_len, batch):
    n_rows = u.shape[0]
    rows = SSD_BLOCK
    n_cb, n_lb = ctx_len // rows, seq // rows
    steps = n_cb + n_lb

    def pos(reverse):
        return functools.partial(_ssd_position, n_cb=n_cb, n_lb=n_lb, batch=batch, reverse=reverse)

    def block_rows(width, reverse):
        return pl.BlockSpec((rows, width), lambda b, s: (pos(reverse)(s, b)[0], 0))

    def small(n, width):
        return pl.BlockSpec((n, width), lambda b, s: (0, 0))

    pad = DT_PAD - 2 * SSD_HEADS
    dtb_row = jnp.pad(dt_bias.reshape(1, 2 * SSD_HEADS), ((0, 0), (0, pad)))
    alog_row = jnp.pad(a_log.reshape(1, 2 * SSD_HEADS), ((0, 0), (0, pad)))
    state = pltpu.VMEM((SSD_PAIRS, SSD_STATE, V7X_LANES), F32)

    y_f = pl.pallas_call(
        _ssd_fwd_kernel,
        grid=(batch, steps),
        in_specs=[block_rows(SSD_CONV_DIM, False), block_rows(DT_PAD, False), small(1, DT_PAD), small(1, DT_PAD)],
        out_specs=block_rows(SSD_D_INNER, False),
        out_shape=jax.ShapeDtypeStruct((n_rows, SSD_D_INNER), BF16),
        scratch_shapes=[state],
        compiler_params=_cparams(2),
        name="ssd_fwd",
    )(u, dt, dtb_row, alog_row)

    d_row = jnp.repeat(d_skip, SSD_HEADDIM).reshape(1, SSD_D_INNER)
    return pl.pallas_call(
        _ssd_bwd_kernel,
        grid=(batch, steps),
        in_specs=[block_rows(SSD_CONV_DIM, True), block_rows(DT_PAD, True), block_rows(SSD_D_INNER, True),
                  block_rows(SSD_D_INNER, True), small(1, DT_PAD), small(1, DT_PAD),
                  small(1, SSD_D_INNER), small(1, SSD_D_INNER)],
        out_specs=block_rows(SSD_D_INNER, True),
        out_shape=jax.ShapeDtypeStruct((n_rows, SSD_D_INNER), BF16),
        scratch_shapes=[pltpu.VMEM((SSD_CHUNK, SSD_D_INNER), F32), state],
        compiler_params=_cparams(2),
        name="ssd_bwd",
    )(u, dt, z, y_f, dtb_row, alog_row, d_row, norm_w.reshape(1, SSD_D_INNER))


def _merge_ffn_kernel(*refs, n_lat_tiles, with_ctx):
    if with_ctx:
        (s_ref, mod_ref, a_ref, ac_ref, so_ref, g_ref, wa_ref, ws_ref, wo_ref,
         gn_ref, wg_ref, wu_ref, wd_ref, o_ref, act_scr) = refs
        a = jnp.where(pl.program_id(0) < n_lat_tiles, a_ref[...], ac_ref[...])
    else:
        (s_ref, mod_ref, a_ref, so_ref, g_ref, wa_ref, ws_ref, wo_ref,
         gn_ref, wg_ref, wu_ref, wd_ref, o_ref, act_scr) = refs
        a = a_ref[...]
    ya = _dot(a, wa_ref[...])
    ys = _dot(so_ref[...], ws_ref[...])
    g = _sigmoid(g_ref[...].astype(F32))
    merged = (g[:, :D_MODEL] * ya + g[:, D_MODEL:] * ys).astype(BF16)
    x = s_ref[...] + mod_ref[0, 5:6, :] * _dot(merged, wo_ref[...])
    o_ref[...] = _ffn_body(x, mod_ref, gn_ref, wg_ref, wu_ref, wd_ref, act_scr, 6)


def _merge_ffn(s, n_rows, mod, a_lat, a_ctx, s_o, g, wa, ws, wo, g_norm, wg, wu, wd, *, seq, batch):
    tm = TM_MERGE
    n_lat_tiles = batch * seq // tm
    with_ctx = a_ctx is not None

    def rows(width):
        return pl.BlockSpec((tm, width), lambda i: (i, 0))

    attn_specs = [pl.BlockSpec((tm, NA_WIDTH), lambda i: (jnp.minimum(i, n_lat_tiles - 1), 0))]
    attn_args = [a_lat]
    if with_ctx:
        attn_specs.append(pl.BlockSpec((tm, NA_WIDTH), lambda i: (jnp.maximum(i - n_lat_tiles, 0), 0)))
        attn_args.append(a_ctx)
    return pl.pallas_call(
        functools.partial(_merge_ffn_kernel, n_lat_tiles=n_lat_tiles, with_ctx=with_ctx),
        grid=(n_rows // tm,),
        in_specs=[rows(D_MODEL),
                  pl.BlockSpec((1, N_MOD, D_MODEL), lambda i: (jnp.minimum(i * tm // seq, batch), 0, 0)),
                  *attn_specs, rows(SSD_D_INNER), rows(2 * D_MODEL),
                  _resident(), _resident(), _resident(),
                  pl.BlockSpec((1, D_MODEL), lambda i: (0, 0)),
                  _resident(), _resident(), _resident()],
        out_specs=rows(D_MODEL),
        out_shape=jax.ShapeDtypeStruct((n_rows, D_MODEL), F32),
        scratch_shapes=[pltpu.VMEM((tm, D_FF), BF16)],
        compiler_params=_cparams(1),
        name="merge_ffn",
    )(s, mod, *attn_args, s_o, g, wa, ws, wo, g_norm.reshape(1, D_MODEL), wg, wu, wd)


def _rope_tables(seq, pad_rows):
    t = np.arange(seq)
    row = (t // GRID_W).astype(np.float32)
    col = (t % GRID_W).astype(np.float32)
    n_freq = HEAD_DIM // 4
    inv = np.float32(ROPE_BASE) ** (-np.arange(n_freq, dtype=np.float32) / np.float32(n_freq))
    ang = np.concatenate([row[:, None] * inv, col[:, None] * inv], axis=-1).astype(np.float32)
    cos, sin = np.cos(ang), np.sin(ang)
    reps = QK_CHUNK // HEAD_DIM
    cos_t = np.tile(np.concatenate([cos, cos], axis=-1), (1, reps))
    sin_t = np.tile(np.concatenate([-sin, sin], axis=-1), (1, reps))
    cos_t = np.concatenate([cos_t, np.ones((pad_rows, QK_CHUNK), np.float32)], axis=0)
    sin_t = np.concatenate([sin_t, np.zeros((pad_rows, QK_CHUNK), np.float32)], axis=0)
    return jnp.asarray(cos_t, F32), jnp.asarray(sin_t, F32)


def _pack_w_in(w_in_l):
    sizes = (NA_WIDTH, NA_WIDTH, NA_WIDTH, SSD_D_INNER, SSD_CONV_DIM, 2 * SSD_HEADS, 2 * D_MODEL)
    offs = np.concatenate([[0], np.cumsum(sizes)])
    q, k, v, z, xbc, dt, g = (w_in_l[:, offs[i]:offs[i + 1]] for i in range(7))
    pad = jnp.zeros((D_MODEL, DT_PAD - 2 * SSD_HEADS), w_in_l.dtype)
    return jnp.concatenate([q, k, v, z, xbc, g, dt, pad], axis=1).astype(BF16)


def kernel(x, c, ctx, c_ctx, w_ada, b_ada, norm_ffn1, ffn1_w_gate, ffn1_w_up, ffn1_w_down, norm_mix, w_in,
           q_norm, k_norm, na_rpb, na_w_o, ssd_conv_w, ssd_conv_b, ssd_dt_bias, ssd_a_log, ssd_d, ssd_norm,
           ssd_w_o, w_out, norm_ffn2, ffn2_w_gate, ffn2_w_up, ffn2_w_down):
    batch, seq, d = x.shape
    ctx_len = ctx.shape[1]
    depth = w_ada.shape[0]
    n_lat = batch * seq
    n_all = n_lat + batch * ctx_len
    assert d == D_MODEL and seq % GRID_W == 0 and seq // GRID_W >= WIN_R
    assert seq % TM_FFN == 0 and (batch * ctx_len) % TM_FFN == 0 and seq % TM_PROJ == 0
    assert ctx_len % SSD_BLOCK == 0 and seq % ctx_len == 0 and ctx_len % GRID_W == 0
    assert (seq // GRID_W) % ATTN_ROWS_PER_STEP == 0 and (batch * ctx_len) % TM_MERGE == 0

    s = jnp.concatenate([x.reshape(n_lat, d), ctx.reshape(batch * ctx_len, d)], axis=0)

    mod_rows = -(-(batch + 1) // V7X_SUBLANES) * V7X_SUBLANES
    v_rows = jnp.zeros((mod_rows, d), F32).at[:batch].set(c).at[batch].set(c_ctx)
    mod_all = _adaln(v_rows, w_ada, b_ada).reshape(depth, mod_rows, N_MOD, d)

    cos_tab, sin_tab = _rope_tables(seq, TM_PROJ)
    head_avg = np.kron(np.eye(QK_CHUNK // HEAD_DIM), np.full((HEAD_DIM, HEAD_DIM), 1.0 / HEAD_DIM))
    gmat = jnp.asarray(np.concatenate([head_avg, head_avg], axis=0), BF16)

    for l in range(depth):
        need_ctx = l < depth - 1
        mod = mod_all[l]
        s = _ffn(s, n_all, mod, norm_ffn1[l], ffn1_w_gate[l].astype(BF16), ffn1_w_up[l].astype(BF16),
                 ffn1_w_down[l].astype(BF16), mod_i=0, seq=seq, batch=batch)
        q_gain = jnp.tile(q_norm[l] * (ATTN_SCALE * LOG2_E), NA_HEADS).reshape(1, NA_WIDTH)
        k_gain = jnp.tile(k_norm[l], NA_HEADS).reshape(1, NA_WIDTH)
        q, k, v, z, u, g, dt = _proj(s, mod, norm_mix[l], _pack_w_in(w_in[l]), gmat, q_gain, k_gain,
                                     cos_tab, sin_tab, ssd_conv_w[l], ssd_conv_b[l],
                                     seq=seq, ctx_len=ctx_len, batch=batch)
        a_lat = _attention(q, k, v, _attention_bias(na_rpb[l]), seq=seq, ctx_len=ctx_len, batch=batch)
        a_ctx = _ctx_attention(q, k, v, seq=seq, ctx_len=ctx_len, batch=batch) if need_ctx else None
        s_o = _ssd(u, dt, z, ssd_dt_bias[l], ssd_a_log[l], ssd_d[l], ssd_norm[l],
                   seq=seq, ctx_len=ctx_len, batch=batch)
        n_out = n_all if need_ctx else n_lat
        s = _merge_ffn(s, n_out, mod, a_lat, a_ctx, s_o, g, na_w_o[l].astype(BF16), ssd_w_o[l].astype(BF16),
                       w_out[l].astype(BF16), norm_ffn2[l], ffn2_w_gate[l].astype(BF16),
                       ffn2_w_up[l].astype(BF16), ffn2_w_down[l].astype(BF16), seq=seq, batch=batch)
    return s[:n_lat].reshape(batch, seq, d)
```

```python
import functools

import numpy as np
import jax
import jax.numpy as jnp
from jax import lax
from jax.experimental import pallas as pl
from jax.experimental.pallas import tpu as pltpu

F32 = jnp.float32
BF16 = jnp.bfloat16

D_MODEL = 1024
GRID_W = 64
HEAD_DIM = 64
NA_HEADS = 16
NA_WIDTH = NA_HEADS * HEAD_DIM
WIN_R = 8
WIN_C = 16
ROPE_BASE = 10000.0
ATTN_SCALE = HEAD_DIM ** -0.5
SSD_D_INNER = 2 * D_MODEL
SSD_HEADDIM = 64
SSD_HEADS = SSD_D_INNER // SSD_HEADDIM
SSD_GROUPS = 4
SSD_STATE = 128
SSD_CONV = 5
SSD_CHUNK = 128
SSD_BC = SSD_GROUPS * SSD_STATE
SSD_CONV_DIM = SSD_D_INNER + 2 * SSD_BC
D_FF = 2816
N_MOD = 9
EPS = 1e-6

V7X_LANES = 128
V7X_SUBLANES = 8
V7X_MXU_DIM = 256
V7X_VMEM_LIMIT_BYTES = 56 * 1024 * 1024

HEAD_PAIR = V7X_LANES // HEAD_DIM
N_PAIRS = NA_HEADS // HEAD_PAIR
SSD_PAIRS = SSD_HEADS // HEAD_PAIR
PAIRS_PER_GROUP = SSD_PAIRS // SSD_GROUPS
DT_PAD = V7X_LANES
MASK_BIAS = -1e30
LOG2_E = 1.4426950408889634

TM_FFN = 1024
TM_PROJ = 512
TM_MERGE = 512
ATTN_ROWS_PER_STEP = 32
ATTN_GROUP_ROWS = 32
SSD_BLOCK = 2 * SSD_CHUNK
PROJ_HALO = 2 * V7X_SUBLANES
CONV_SEG = 256
FF_CHUNK = V7X_MXU_DIM
QK_CHUNK = V7X_MXU_DIM
PROJ_CHUNK = 512

OFF_Q = 0
OFF_K = OFF_Q + NA_WIDTH
OFF_V = OFF_K + NA_WIDTH
OFF_Z = OFF_V + NA_WIDTH
OFF_XBC = OFF_Z + SSD_D_INNER
OFF_G = OFF_XBC + SSD_CONV_DIM
OFF_DT = OFF_G + 2 * D_MODEL
N_IN_PACKED = OFF_DT + DT_PAD


def _cparams(n_axes):
    return pltpu.CompilerParams(dimension_semantics=("arbitrary",) * n_axes,
                                vmem_limit_bytes=V7X_VMEM_LIMIT_BYTES)


def _resident():
    return pl.BlockSpec(memory_space=pltpu.VMEM)


def _sigmoid(x):
    return 0.5 * jnp.tanh(0.5 * x) + 0.5


def _silu(x):
    h = 0.5 * x
    return h + h * jnp.tanh(h)


def _softplus(x):
    return jnp.maximum(x, 0.0) + jnp.log(1.0 + jnp.exp(-jnp.abs(x)))


def _dot(a, b):
    return jnp.dot(a, b, preferred_element_type=F32)


def _dot_nt(a, b):
    return lax.dot_general(a, b, (((1,), (1,)), ((), ())), preferred_element_type=F32)


def _norm_modulate(x, g_row, shift, scale):
    ms = jnp.mean(x * x, axis=-1, keepdims=True)
    y = x * lax.rsqrt(ms + EPS) * g_row
    return y * (1.0 + scale) + shift


def _adaln_kernel(v_ref, w_ref, b_ref, o_ref):
    sv = _silu(v_ref[...]).astype(BF16)
    o_ref[0] = _dot(sv, w_ref[0].astype(BF16)) + b_ref[0]


def _adaln(v_rows, w_ada, b_ada):
    depth, d, n = w_ada.shape
    rows = v_rows.shape[0]
    tn = D_MODEL
    return pl.pallas_call(
        _adaln_kernel,
        grid=(depth, n // tn),
        in_specs=[pl.BlockSpec((rows, d), lambda l, j: (0, 0)),
                  pl.BlockSpec((1, d, tn), lambda l, j: (l, 0, j)),
                  pl.BlockSpec((1, 1, tn), lambda l, j: (l, 0, j))],
        out_specs=pl.BlockSpec((1, rows, tn), lambda l, j: (l, 0, j)),
        out_shape=jax.ShapeDtypeStruct((depth, rows, n), F32),
        compiler_params=_cparams(2),
        name="adaln",
    )(v_rows, w_ada, b_ada.reshape(depth, 1, n))


def _ffn_body(x, mod_ref, g_ref, wg_ref, wu_ref, wd_ref, act_scr, mod_i):
    shift = mod_ref[0, mod_i:mod_i + 1, :]
    scale = mod_ref[0, mod_i + 1:mod_i + 2, :]
    gate = mod_ref[0, mod_i + 2:mod_i + 3, :]
    h = _norm_modulate(x, g_ref[...], shift, scale).astype(BF16)
    for j in range(D_FF // FF_CHUNK):
        cs = slice(j * FF_CHUNK, (j + 1) * FF_CHUNK)
        act_scr[:, cs] = (_silu(_dot(h, wg_ref[:, cs])) * _dot(h, wu_ref[:, cs])).astype(BF16)
    y = _dot(act_scr[...], wd_ref[...])
    return x + 0.5 * gate * y


def _ffn_kernel(s_ref, mod_ref, g_ref, wg_ref, wu_ref, wd_ref, o_ref, act_scr, *, mod_i):
    o_ref[...] = _ffn_body(s_ref[...], mod_ref, g_ref, wg_ref, wu_ref, wd_ref, act_scr, mod_i)


def _ffn(s, n_rows, mod, g_norm, wg, wu, wd, *, mod_i, seq, batch):
    tm = TM_FFN
    return pl.pallas_call(
        functools.partial(_ffn_kernel, mod_i=mod_i),
        grid=(n_rows // tm,),
        in_specs=[pl.BlockSpec((tm, D_MODEL), lambda i: (i, 0)),
                  pl.BlockSpec((1, N_MOD, D_MODEL), lambda i: (jnp.minimum(i * tm // seq, batch), 0, 0)),
                  pl.BlockSpec((1, D_MODEL), lambda i: (0, 0)),
                  _resident(), _resident(), _resident()],
        out_specs=pl.BlockSpec((tm, D_MODEL), lambda i: (i, 0)),
        out_shape=jax.ShapeDtypeStruct((n_rows, D_MODEL), F32),
        scratch_shapes=[pltpu.VMEM((tm, D_FF), BF16)],
        compiler_params=_cparams(1),
        name="ffn",
    )(s, mod, g_norm.reshape(1, D_MODEL), wg, wu, wd)


def _proj_kernel(s_ref, sp_ref, sn_ref, mod_ref, g_ref, w_ref, gmat_ref, qg_ref, kg_ref, cos_ref, sin_ref,
                 cw_ref, cbias_ref, q_ref, k_ref, v_ref, z_ref, u_ref, gg_ref, dt_ref, h_scr, slab_scr,
                 *, n_lat_tiles, seq, ctx_len):
    tm = s_ref.shape[0]

    def hidden(x):
        return _norm_modulate(x, g_ref[...], mod_ref[0, 3:4, :], mod_ref[0, 4:5, :]).astype(BF16)

    h_scr[0:PROJ_HALO, :] = hidden(sp_ref[...])
    h_scr[PROJ_HALO:PROJ_HALO + tm, :] = hidden(s_ref[...])
    h_scr[PROJ_HALO + tm:, :] = hidden(sn_ref[...])

    def h_main():
        return h_scr[PROJ_HALO:PROJ_HALO + tm, :]

    lane = lax.broadcasted_iota(jnp.int32, (tm, QK_CHUNK), 1)
    first_half = (lane % HEAD_DIM) < (HEAD_DIM // 2)
    cos = cos_ref[...]
    sin = sin_ref[...]

    def qk_raw(off, j):
        y = _dot(h_main(), w_ref[:, off + j * QK_CHUNK:off + (j + 1) * QK_CHUNK])
        return y, (y * y).astype(BF16)

    def qk_finish(gain_ref, out_ref, j, y, sq_split):
        cs = slice(j * QK_CHUNK, (j + 1) * QK_CHUNK)
        ms = _dot(sq_split, gmat_ref[...])
        yn = y * lax.rsqrt(ms + EPS) * gain_ref[:, cs]
        swapped = jnp.where(first_half,
                            pltpu.roll(yn, QK_CHUNK - HEAD_DIM // 2, 1),
                            pltpu.roll(yn, HEAD_DIM // 2, 1))
        out_ref[:, cs] = (yn * cos + swapped * sin).astype(out_ref.dtype)

    def plain_chunk(off, out_ref, j):
        y = _dot(h_main(), w_ref[:, off + j * PROJ_CHUNK:off + (j + 1) * PROJ_CHUNK])
        out_ref[:, j * PROJ_CHUNK:(j + 1) * PROJ_CHUNK] = y.astype(out_ref.dtype)

    tile = pl.program_id(0)
    is_ctx = tile >= n_lat_tiles
    row0 = jnp.where(is_ctx, tile - n_lat_tiles, tile) * tm
    seq_len = jnp.where(is_ctx, ctx_len, seq)
    seg_starts = [row0 + sg * CONV_SEG for sg in range(tm // CONV_SEG)]
    zero_prev = [lax.rem(st, seq_len) == 0 for st in seg_starts]
    zero_next = [lax.rem(st + CONV_SEG, seq_len) == 0 for st in seg_starts]

    def conv_chunk(j):
        y = _dot(h_scr[...], w_ref[:, OFF_XBC + j * PROJ_CHUNK:OFF_XBC + (j + 1) * PROJ_CHUNK])
        par = j % 2
        for lb in range(PROJ_CHUNK // V7X_LANES):
            cs = slice(j * PROJ_CHUNK + lb * V7X_LANES, j * PROJ_CHUNK + (lb + 1) * V7X_LANES)
            ylb = y[:, lb * V7X_LANES:(lb + 1) * V7X_LANES]
            for sg in range(tm // CONV_SEG):
                lo = sg * CONV_SEG
                slab_scr[par, lb, sg, 0:PROJ_HALO, :] = jnp.where(zero_prev[sg], 0.0, ylb[lo:lo + PROJ_HALO])
                slab_scr[par, lb, sg, PROJ_HALO:PROJ_HALO + CONV_SEG, :] = ylb[lo + PROJ_HALO:lo + PROJ_HALO + CONV_SEG]
                slab_scr[par, lb, sg, PROJ_HALO + CONV_SEG:, :] = jnp.where(
                    zero_next[sg], 0.0, ylb[lo + PROJ_HALO + CONV_SEG:lo + 2 * PROJ_HALO + CONV_SEG])
                acc = jnp.broadcast_to(cbias_ref[:, cs], (CONV_SEG, V7X_LANES))
                for t in range(SSD_CONV):
                    r0 = PROJ_HALO - SSD_CONV // 2 + t
                    acc = acc + slab_scr[par, lb, sg, r0:r0 + CONV_SEG, :] * cw_ref[t:t + 1, cs]
                u_ref[lo:lo + CONV_SEG, cs] = _silu(acc).astype(u_ref.dtype)

    qk_chunks = [(off, gain, out, j) for off, gain, out in ((OFF_Q, qg_ref, q_ref), (OFF_K, kg_ref, k_ref))
                 for j in range(NA_WIDTH // QK_CHUNK)]
    plain_tasks = [functools.partial(plain_chunk, off, out, j)
                   for off, width, out in ((OFF_V, NA_WIDTH, v_ref), (OFF_Z, SSD_D_INNER, z_ref),
                                           (OFF_G, 2 * D_MODEL, gg_ref))
                   for j in range(width // PROJ_CHUNK)]
    conv_tasks = [functools.partial(conv_chunk, j) for j in range(SSD_CONV_DIM // PROJ_CHUNK)]
    plain_tasks = [t for pair in zip(plain_tasks[:len(conv_tasks)], conv_tasks) for t in pair] \
        + plain_tasks[len(conv_tasks):]
    assert len(plain_tasks) == 2 * len(qk_chunks)
    for i, (off, gain, out, j) in enumerate(qk_chunks):
        y, sq_split = qk_raw(off, j)
        plain_tasks[2 * i]()
        qk_finish(gain, out, j, y, sq_split)
        plain_tasks[2 * i + 1]()
    dt_ref[...] = _dot(h_main(), w_ref[:, OFF_DT:OFF_DT + DT_PAD])


def _proj(s, mod, g_norm, w_packed, gmat, q_gain, k_gain, cos_tab, sin_tab, conv_w, conv_b, *, seq, ctx_len,
          batch):
    n_rows = s.shape[0]
    tm = TM_PROJ
    n_lat = batch * seq // tm
    tiles_per_seq = seq // tm
    halo_per_tile = tm // PROJ_HALO
    n_halo_blocks = n_rows // PROJ_HALO

    def rope_idx(i):
        return (jnp.where(i < n_lat, i % tiles_per_seq, tiles_per_seq), 0)

    def rows(width):
        return pl.BlockSpec((tm, width), lambda i: (i, 0))

    def const(n, width):
        return pl.BlockSpec((n, width), lambda i: (0, 0))

    def out(width, dtype):
        return jax.ShapeDtypeStruct((n_rows, width), dtype)

    conv_w_rows = jnp.zeros((V7X_SUBLANES, SSD_CONV_DIM), F32).at[:SSD_CONV].set(conv_w)
    slabs = pltpu.VMEM((2, PROJ_CHUNK // V7X_LANES, tm // CONV_SEG, CONV_SEG + 2 * PROJ_HALO, V7X_LANES), F32)
    return pl.pallas_call(
        functools.partial(_proj_kernel, n_lat_tiles=n_lat, seq=seq, ctx_len=ctx_len),
        grid=(n_rows // tm,),
        in_specs=[rows(D_MODEL),
                  pl.BlockSpec((PROJ_HALO, D_MODEL), lambda i: (jnp.maximum(i * halo_per_tile - 1, 0), 0)),
                  pl.BlockSpec((PROJ_HALO, D_MODEL),
                               lambda i: (jnp.minimum((i + 1) * halo_per_tile, n_halo_blocks - 1), 0)),
                  pl.BlockSpec((1, N_MOD, D_MODEL), lambda i: (jnp.minimum(i * tm // seq, batch), 0, 0)),
                  const(1, D_MODEL),
                  _resident(), _resident(),
                  const(1, NA_WIDTH), const(1, NA_WIDTH),
                  pl.BlockSpec((tm, QK_CHUNK), rope_idx),
                  pl.BlockSpec((tm, QK_CHUNK), rope_idx),
                  const(V7X_SUBLANES, SSD_CONV_DIM), const(1, SSD_CONV_DIM)],
        out_specs=[rows(NA_WIDTH), rows(NA_WIDTH), rows(NA_WIDTH), rows(SSD_D_INNER),
                   rows(SSD_CONV_DIM), rows(2 * D_MODEL), rows(DT_PAD)],
        out_shape=[out(NA_WIDTH, BF16), out(NA_WIDTH, BF16), out(NA_WIDTH, BF16), out(SSD_D_INNER, BF16),
                   out(SSD_CONV_DIM, BF16), out(2 * D_MODEL, BF16), out(DT_PAD, F32)],
        scratch_shapes=[pltpu.VMEM((tm + 2 * PROJ_HALO, D_MODEL), BF16), slabs],
        compiler_params=_cparams(1),
        name="in_proj",
    )(s, s, s, mod, g_norm.reshape(1, D_MODEL), w_packed, gmat, q_gain, k_gain, cos_tab, sin_tab,
      conv_w_rows, conv_b.reshape(1, SSD_CONV_DIM))


ATTN_STACK = HEAD_PAIR * GRID_W


def _stacked_queries(q_ref, n_grid_rows):
    low = lax.broadcasted_iota(jnp.int32, (GRID_W, V7X_LANES), 1) < HEAD_DIM
    parts = []
    for rr in range(n_grid_rows):
        qrow = q_ref[rr * GRID_W:(rr + 1) * GRID_W, :]
        zero = jnp.zeros_like(qrow)
        parts += [jnp.where(low, qrow, zero), jnp.where(low, zero, qrow)]
    return jnp.concatenate(parts, axis=0)


def _unstack_heads(o, denom, o_ref, n_grid_rows):
    low = lax.broadcasted_iota(jnp.int32, (GRID_W, V7X_LANES), 1) < HEAD_DIM
    o = o / denom
    rows = [jnp.where(low, o[rr * ATTN_STACK:rr * ATTN_STACK + GRID_W],
                      o[rr * ATTN_STACK + GRID_W:(rr + 1) * ATTN_STACK]) for rr in range(n_grid_rows)]
    o_ref[...] = jnp.concatenate(rows, axis=0).astype(o_ref.dtype)


def _attn_kernel(q_ref, k_ref, v_ref, kc_ref, vc_ref, bias_ref, o_ref, *, rows_per_step, grid_rows):
    rg = pl.program_id(2)
    win_keys = WIN_R * GRID_W
    qs = _stacked_queries(q_ref, rows_per_step)

    def window_start(rr):
        r = rg * rows_per_step + rr
        r0 = jnp.clip(r - WIN_R // 2, 0, grid_rows - WIN_R)
        return r, r0, pl.multiple_of(r0 * GRID_W, GRID_W)

    gs = ATTN_GROUP_ROWS

    def scores(g):
        sws = []
        for rr in range(g * gs, (g + 1) * gs):
            r, r0, start = window_start(rr)
            d0 = (WIN_R - 1) - (r - r0)
            bias = jnp.concatenate([bias_ref[0, d0 + i] for i in range(0, WIN_R, HEAD_PAIR)], axis=1)
            sws.append(_dot_nt(qs[rr * ATTN_STACK:(rr + 1) * ATTN_STACK], k_ref[pl.ds(start, win_keys), :]) + bias)
        sc = _dot_nt(qs[g * gs * ATTN_STACK:(g + 1) * gs * ATTN_STACK], kc_ref[...])
        return jnp.concatenate(sws, axis=0), sc

    def probs(sw, sc):
        m = jnp.maximum(jnp.max(sc, axis=-1, keepdims=True), jnp.max(sw, axis=-1, keepdims=True))
        return jnp.exp2((sw - m).astype(BF16)), jnp.exp2((sc - m).astype(BF16))

    def weighted(g, pw, pc):
        ones_w = jnp.ones((win_keys, V7X_LANES), BF16)
        ones_c = jnp.ones((kc_ref.shape[0], V7X_LANES), BF16)
        ows = []
        for i, rr in enumerate(range(g * gs, (g + 1) * gs)):
            _, _, start = window_start(rr)
            v_aug = jnp.concatenate([v_ref[pl.ds(start, win_keys), :], ones_w], axis=1)
            ows.append(_dot(pw[i * ATTN_STACK:(i + 1) * ATTN_STACK], v_aug))
        return jnp.concatenate(ows, axis=0) + _dot(pc, jnp.concatenate([vc_ref[...], ones_c], axis=1))

    outs = []
    for g in range(rows_per_step // gs):
        outs.append(weighted(g, *probs(*scores(g))))
    o_aug = jnp.concatenate(outs, axis=0)
    _unstack_heads(o_aug[:, :V7X_LANES], o_aug[:, V7X_LANES:], o_ref, rows_per_step)


def _attention(q, k, v, bias, layer, *, seq, ctx_len, batch):
    rows_per_step = ATTN_ROWS_PER_STEP
    tq = rows_per_step * GRID_W
    grid_rows = seq // GRID_W
    n_rg = grid_rows // rows_per_step
    ctx_blk0 = batch * seq // ctx_len
    return pl.pallas_call(
        functools.partial(_attn_kernel, rows_per_step=rows_per_step, grid_rows=grid_rows),
        grid=(N_PAIRS, batch, n_rg),
        in_specs=[pl.BlockSpec((tq, V7X_LANES), lambda j, b, rg: (b * n_rg + rg, j)),
                  pl.BlockSpec((seq, V7X_LANES), lambda j, b, rg: (b, j)),
                  pl.BlockSpec((seq, V7X_LANES), lambda j, b, rg: (b, j)),
                  pl.BlockSpec((ctx_len, V7X_LANES), lambda j, b, rg: (ctx_blk0 + b, j)),
                  pl.BlockSpec((ctx_len, V7X_LANES), lambda j, b, rg: (ctx_blk0 + b, j)),
                  pl.BlockSpec((1, 2 * WIN_R - 2, ATTN_STACK, 2 * GRID_W), lambda j, b, rg: (layer * N_PAIRS + j, 0, 0, 0))],
        out_specs=pl.BlockSpec((tq, V7X_LANES), lambda j, b, rg: (b * n_rg + rg, j)),
        out_shape=jax.ShapeDtypeStruct((batch * seq, NA_WIDTH), BF16),
        compiler_params=_cparams(3),
        name="attention",
    )(q, k, v, k, v, bias)


def _ctx_attn_kernel(q_ref, kc_ref, vc_ref, o_ref, *, n_grid_rows):
    qs = _stacked_queries(q_ref, n_grid_rows)
    sc = _dot_nt(qs, kc_ref[...])
    pc = jnp.exp2(sc - jnp.max(sc, axis=-1, keepdims=True))
    _unstack_heads(_dot(pc.astype(BF16), vc_ref[...]), jnp.sum(pc, axis=-1, keepdims=True), o_ref, n_grid_rows)


def _ctx_attention(q, k, v, *, seq, ctx_len, batch):
    ctx_blk0 = batch * seq // ctx_len
    spec = pl.BlockSpec((ctx_len, V7X_LANES), lambda j, b: (ctx_blk0 + b, j))
    return pl.pallas_call(
        functools.partial(_ctx_attn_kernel, n_grid_rows=ctx_len // GRID_W),
        grid=(N_PAIRS, batch),
        in_specs=[spec, spec, spec],
        out_specs=pl.BlockSpec((ctx_len, V7X_LANES), lambda j, b: (b, j)),
        out_shape=jax.ShapeDtypeStruct((batch * ctx_len, NA_WIDTH), BF16),
        compiler_params=_cparams(2),
        name="ctx_attention",
    )(q, k, v)


def _attention_bias(rpb_all):
    n_layers = rpb_all.shape[0]
    n_dr, n_dc = 2 * WIN_R - 1, 2 * WIN_C - 1
    c = np.arange(GRID_W)[:, None]
    kc = np.arange(GRID_W)[None, :]
    s_col = np.clip(c - WIN_C // 2, 0, GRID_W - WIN_C)
    valid = (kc >= s_col) & (kc < s_col + WIN_C)
    dc = kc - c + (WIN_C - 1)
    onehot = ((dc[None] == np.arange(n_dc)[:, None, None]) & valid[None]).astype(np.float32)
    masked = np.where(valid, 0.0, MASK_BIAS).astype(np.float32)
    r = (rpb_all.astype(F32) * LOG2_E).reshape(n_layers, N_PAIRS, HEAD_PAIR, n_dr, n_dc)
    r2 = jnp.stack([r[:, :, :, :-1], r[:, :, :, 1:]], axis=4)
    out = jnp.einsum('lphdex,xck->lpdhcek', r2, jnp.asarray(onehot), precision=lax.Precision.HIGHEST)
    out = out + jnp.asarray(masked)[:, None, :]
    return out.reshape(n_layers * N_PAIRS, n_dr - 1, HEAD_PAIR * GRID_W, 2 * GRID_W)


def _scan_rows(x, reverse):
    n = x.shape[0]
    row = lax.broadcasted_iota(jnp.int32, x.shape, 0)
    step = 1
    while step < n:
        if reverse:
            x = x + jnp.where(row < n - step, pltpu.roll(x, n - step, 0), 0.0)
        else:
            x = x + jnp.where(row >= step, pltpu.roll(x, step, 0), 0.0)
        step *= 2
    return x


def _ssd_chunk(get_x, get_b, get_c, dt_raw, dtb_ref, alog_ref, st_scr, write_y, *, reverse):
    q = SSD_CHUNK
    direction = 1 if reverse else 0
    dt = _softplus(dt_raw + dtb_ref[...])
    a = -jnp.exp(alog_ref[...])
    ac = _scan_rows(dt * a, reverse)
    edge = 0 if reverse else q - 1
    e_all = jnp.exp(ac)
    wdt_t = (jnp.exp(ac[edge:edge + 1, :] - ac) * dt).T
    ac2 = ac * LOG2_E
    lg2_t = (ac2 - jnp.log2(dt)).T
    ii = lax.broadcasted_iota(jnp.int32, (q, q), 0)
    jj = lax.broadcasted_iota(jnp.int32, (q, q), 1)
    causal = (ii <= jj) if reverse else (ii >= jj)
    low = jj < SSD_HEADDIM
    for g in range(SSD_GROUPS):
        b_g = get_b(g)
        c_g = get_c(g)
        cb = _dot_nt(c_g.astype(BF16), b_g.astype(BF16))
        b_t = b_g.T
        for pp in range(PAIRS_PER_GROUP):
            pair = g * PAIRS_PER_GROUP + pp
            xp = get_x(pair)
            st = st_scr[pair]
            rhs = jnp.concatenate([xp, st.astype(BF16)], axis=0)
            zero = jnp.zeros_like(xp)
            x_split = jnp.concatenate([jnp.where(low, xp, zero), jnp.where(low, zero, xp)], axis=0)
            ys, bws, decays = [], [], []
            for half in range(HEAD_PAIR):
                hl = direction * SSD_HEADS + pair * HEAD_PAIR + half
                a_col = jnp.broadcast_to(ac2[:, hl:hl + 1], (q, q))
                e_col = jnp.broadcast_to(e_all[:, hl:hl + 1], (q, q))
                m = jnp.exp2(jnp.where(causal, a_col - lg2_t[hl:hl + 1, :], -jnp.inf)) * cb
                lhs = jnp.concatenate([m, c_g * e_col], axis=1).astype(BF16)
                ys.append(_dot(lhs, rhs))
                bws.append((b_t * wdt_t[hl:hl + 1, :]).astype(BF16))
                decays.append(e_col[edge:edge + 1, :])
            write_y(pair, jnp.where(low, ys[0], ys[1]))
            new = _dot(jnp.concatenate(bws, axis=1), x_split)
            st_scr[pair] = st * jnp.where(low[0:1], decays[0], decays[1]) + new


def _ssd_position(step, b, *, n_cb, n_lb, batch, reverse):
    is_ctx = step < n_cb
    loc = jnp.where(is_ctx, step, step - n_cb)
    if reverse:
        loc = jnp.where(is_ctx, n_cb - 1 - loc, n_lb - 1 - loc)
    blk = jnp.where(is_ctx, batch * n_lb + b * n_cb + loc, b * n_lb + loc)
    return blk, loc


def _ssd_chunk_from_u(u_ref, rs, dt_ref, dtb_ref, alog_ref, st_scr, write_y, *, reverse):
    def get_bc(g, off):
        return u_ref[rs, SSD_D_INNER + off + g * SSD_STATE:SSD_D_INNER + off + (g + 1) * SSD_STATE].astype(F32)

    _ssd_chunk(lambda pair: u_ref[rs, pair * V7X_LANES:(pair + 1) * V7X_LANES],
               lambda g: get_bc(g, 0), lambda g: get_bc(g, SSD_BC),
               dt_ref[rs, :], dtb_ref, alog_ref, st_scr, write_y, reverse=reverse)


def _ssd_fwd_kernel(u_ref, dt_ref, dtb_ref, alog_ref, y_ref, st_scr):
    step = pl.program_id(1)

    @pl.when(step == 0)
    def _():
        st_scr[...] = jnp.zeros_like(st_scr)

    for ci in range(SSD_BLOCK // SSD_CHUNK):
        rs = slice(ci * SSD_CHUNK, (ci + 1) * SSD_CHUNK)

        def write_y(pair, val, rs=rs):
            y_ref[rs, pair * V7X_LANES:(pair + 1) * V7X_LANES] = val.astype(y_ref.dtype)

        _ssd_chunk_from_u(u_ref, rs, dt_ref, dtb_ref, alog_ref, st_scr, write_y, reverse=False)


def _ssd_bwd_kernel(u_ref, dt_ref, z_ref, yf_ref, dtb_ref, alog_ref, dskip_ref, nw_ref,
                    so_ref, y_scr, st_scr):
    step = pl.program_id(1)

    @pl.when(step == 0)
    def _():
        st_scr[...] = jnp.zeros_like(st_scr)

    gw = SSD_D_INNER // SSD_GROUPS
    for ci in reversed(range(SSD_BLOCK // SSD_CHUNK)):
        rs = slice(ci * SSD_CHUNK, (ci + 1) * SSD_CHUNK)

        def write_y(pair, val):
            y_scr[:, pair * V7X_LANES:(pair + 1) * V7X_LANES] = val

        _ssd_chunk_from_u(u_ref, rs, dt_ref, dtb_ref, alog_ref, st_scr, write_y, reverse=True)
        for g in range(SSD_GROUPS):
            cs = slice(g * gw, (g + 1) * gw)
            y = yf_ref[rs, cs].astype(F32) + y_scr[:, cs] + dskip_ref[:, cs] * u_ref[rs, cs].astype(F32)
            y = y * _silu(z_ref[rs, cs].astype(F32))
            ms = jnp.mean(y * y, axis=-1, keepdims=True)
            so_ref[rs, cs] = (y * lax.rsqrt(ms + EPS) * nw_ref[:, cs]).astype(so_ref.dtype)


def _ssd(u, dt, z, dt_bias, a_log, d_skip, norm_w, *, seq, ctx_len, batch):
    n_rows = u.shape[0]
    rows = SSD_BLOCK
    n_cb, n_lb = ctx_len // rows, seq // rows
    steps = n_cb + n_lb

    def pos(reverse):
        return functools.partial(_ssd_position, n_cb=n_cb, n_lb=n_lb, batch=batch, reverse=reverse)

    def block_rows(width, reverse):
        return pl.BlockSpec((rows, width), lambda b, s: (pos(reverse)(s, b)[0], 0))

    def small(n, width):
        return pl.BlockSpec((n, width), lambda b, s: (0, 0))

    pad = DT_PAD - 2 * SSD_HEADS
    dtb_row = jnp.pad(dt_bias.reshape(1, 2 * SSD_HEADS), ((0, 0), (0, pad)))
    alog_row = jnp.pad(a_log.reshape(1, 2 * SSD_HEADS), ((0, 0), (0, pad)))
    state = pltpu.VMEM((SSD_PAIRS, SSD_STATE, V7X_LANES), F32)

    y_f = pl.pallas_call(
        _ssd_fwd_kernel,
        grid=(batch, steps),
        in_specs=[block_rows(SSD_CONV_DIM, False), block_rows(DT_PAD, False), small(1, DT_PAD), small(1, DT_PAD)],
        out_specs=block_rows(SSD_D_INNER, False),
        out_shape=jax.ShapeDtypeStruct((n_rows, SSD_D_INNER), BF16),
        scratch_shapes=[state],
        compiler_params=_cparams(2),
        name="ssd_fwd",
    )(u, dt, dtb_row, alog_row)

    d_row = jnp.repeat(d_skip, SSD_HEADDIM).reshape(1, SSD_D_INNER)
    return pl.pallas_call(
        _ssd_bwd_kernel,
        grid=(batch, steps),
        in_specs=[block_rows(SSD_CONV_DIM, True), block_rows(DT_PAD, True), block_rows(SSD_D_INNER, True),
                  block_rows(SSD_D_INNER, True), small(1, DT_PAD), small(1, DT_PAD),
                  small(1, SSD_D_INNER), small(1, SSD_D_INNER)],
        out_specs=block_rows(SSD_D_INNER, True),
        out_shape=jax.ShapeDtypeStruct((n_rows, SSD_D_INNER), BF16),
        scratch_shapes=[pltpu.VMEM((SSD_CHUNK, SSD_D_INNER), F32), state],
        compiler_params=_cparams(2),
        name="ssd_bwd",
    )(u, dt, z, y_f, dtb_row, alog_row, d_row, norm_w.reshape(1, SSD_D_INNER))


def _merge_ffn_kernel(*refs, n_lat_tiles, with_ctx):
    if with_ctx:
        (s_ref, mod_ref, a_ref, ac_ref, so_ref, g_ref, wa_ref, ws_ref, wo_ref,
         gn_ref, wg_ref, wu_ref, wd_ref, o_ref, act_scr) = refs
        a = jnp.where(pl.program_id(0) < n_lat_tiles, a_ref[...], ac_ref[...])
    else:
        (s_ref, mod_ref, a_ref, so_ref, g_ref, wa_ref, ws_ref, wo_ref,
         gn_ref, wg_ref, wu_ref, wd_ref, o_ref, act_scr) = refs
        a = a_ref[...]
    ya = _dot(a, wa_ref[...])
    ys = _dot(so_ref[...], ws_ref[...])
    g = _sigmoid(g_ref[...].astype(F32))
    merged = (g[:, :D_MODEL] * ya + g[:, D_MODEL:] * ys).astype(BF16)
    x = s_ref[...] + mod_ref[0, 5:6, :] * _dot(merged, wo_ref[...])
    o_ref[...] = _ffn_body(x, mod_ref, gn_ref, wg_ref, wu_ref, wd_ref, act_scr, 6)


def _merge_ffn(s, n_rows, mod, a_lat, a_ctx, s_o, g, wa, ws, wo, g_norm, wg, wu, wd, *, seq, batch):
    tm = TM_MERGE
    n_lat_tiles = batch * seq // tm
    with_ctx = a_ctx is not None

    def rows(width):
        return pl.BlockSpec((tm, width), lambda i: (i, 0))

    attn_specs = [pl.BlockSpec((tm, NA_WIDTH), lambda i: (jnp.minimum(i, n_lat_tiles - 1), 0))]
    attn_args = [a_lat]
    if with_ctx:
        attn_specs.append(pl.BlockSpec((tm, NA_WIDTH), lambda i: (jnp.maximum(i - n_lat_tiles, 0), 0)))
        attn_args.append(a_ctx)
    return pl.pallas_call(
        functools.partial(_merge_ffn_kernel, n_lat_tiles=n_lat_tiles, with_ctx=with_ctx),
        grid=(n_rows // tm,),
        in_specs=[rows(D_MODEL),
                  pl.BlockSpec((1, N_MOD, D_MODEL), lambda i: (jnp.minimum(i * tm // seq, batch), 0, 0)),
                  *attn_specs, rows(SSD_D_INNER), rows(2 * D_MODEL),
                  _resident(), _resident(), _resident(),
                  pl.BlockSpec((1, D_MODEL), lambda i: (0, 0)),
                  _resident(), _resident(), _resident()],
        out_specs=rows(D_MODEL),
        out_shape=jax.ShapeDtypeStruct((n_rows, D_MODEL), F32),
        scratch_shapes=[pltpu.VMEM((tm, D_FF), BF16)],
        compiler_params=_cparams(1),
        name="merge_ffn",
    )(s, mod, *attn_args, s_o, g, wa, ws, wo, g_norm.reshape(1, D_MODEL), wg, wu, wd)


def _rope_tables(seq, pad_rows):
    t = np.arange(seq)
    row = (t // GRID_W).astype(np.float32)
    col = (t % GRID_W).astype(np.float32)
    n_freq = HEAD_DIM // 4
    inv = np.float32(ROPE_BASE) ** (-np.arange(n_freq, dtype=np.float32) / np.float32(n_freq))
    ang = np.concatenate([row[:, None] * inv, col[:, None] * inv], axis=-1).astype(np.float32)
    cos, sin = np.cos(ang), np.sin(ang)
    reps = QK_CHUNK // HEAD_DIM
    cos_t = np.tile(np.concatenate([cos, cos], axis=-1), (1, reps))
    sin_t = np.tile(np.concatenate([-sin, sin], axis=-1), (1, reps))
    cos_t = np.concatenate([cos_t, np.ones((pad_rows, QK_CHUNK), np.float32)], axis=0)
    sin_t = np.concatenate([sin_t, np.zeros((pad_rows, QK_CHUNK), np.float32)], axis=0)
    return jnp.asarray(cos_t, F32), jnp.asarray(sin_t, F32)


def _pack_w_in(w_in_l):
    sizes = (NA_WIDTH, NA_WIDTH, NA_WIDTH, SSD_D_INNER, SSD_CONV_DIM, 2 * SSD_HEADS, 2 * D_MODEL)
    offs = np.concatenate([[0], np.cumsum(sizes)])
    q, k, v, z, xbc, dt, g = (w_in_l[:, offs[i]:offs[i + 1]] for i in range(7))
    pad = jnp.zeros((D_MODEL, DT_PAD - 2 * SSD_HEADS), w_in_l.dtype)
    return jnp.concatenate([q, k, v, z, xbc, g, dt, pad], axis=1).astype(BF16)


def kernel(x, c, ctx, c_ctx, w_ada, b_ada, norm_ffn1, ffn1_w_gate, ffn1_w_up, ffn1_w_down, norm_mix, w_in,
           q_norm, k_norm, na_rpb, na_w_o, ssd_conv_w, ssd_conv_b, ssd_dt_bias, ssd_a_log, ssd_d, ssd_norm,
           ssd_w_o, w_out, norm_ffn2, ffn2_w_gate, ffn2_w_up, ffn2_w_down):
    batch, seq, d = x.shape
    ctx_len = ctx.shape[1]
    depth = w_ada.shape[0]
    n_lat = batch * seq
    n_all = n_lat + batch * ctx_len
    assert d == D_MODEL and seq % GRID_W == 0 and seq // GRID_W >= WIN_R
    assert seq % TM_FFN == 0 and (batch * ctx_len) % TM_FFN == 0 and seq % TM_PROJ == 0
    assert ctx_len % SSD_BLOCK == 0 and seq % ctx_len == 0 and ctx_len % GRID_W == 0
    assert (seq // GRID_W) % ATTN_ROWS_PER_STEP == 0 and (batch * ctx_len) % TM_MERGE == 0

    s = jnp.concatenate([x.reshape(n_lat, d), ctx.reshape(batch * ctx_len, d)], axis=0)

    mod_rows = -(-(batch + 1) // V7X_SUBLANES) * V7X_SUBLANES
    v_rows = jnp.zeros((mod_rows, d), F32).at[:batch].set(c).at[batch].set(c_ctx)
    mod_all = _adaln(v_rows, w_ada, b_ada).reshape(depth, mod_rows, N_MOD, d)

    cos_tab, sin_tab = _rope_tables(seq, TM_PROJ)
    gmat = jnp.asarray(np.kron(np.eye(QK_CHUNK // HEAD_DIM), np.full((HEAD_DIM, HEAD_DIM), 1.0 / HEAD_DIM)), BF16)

    attn_bias = _attention_bias(na_rpb)

    for l in range(depth):
        need_ctx = l < depth - 1
        mod = mod_all[l]
        s = _ffn(s, n_all, mod, norm_ffn1[l], ffn1_w_gate[l].astype(BF16), ffn1_w_up[l].astype(BF16),
                 ffn1_w_down[l].astype(BF16), mod_i=0, seq=seq, batch=batch)
        q_gain = jnp.tile(q_norm[l] * (ATTN_SCALE * LOG2_E), NA_HEADS).reshape(1, NA_WIDTH)
        k_gain = jnp.tile(k_norm[l], NA_HEADS).reshape(1, NA_WIDTH)
        q, k, v, z, u, g, dt = _proj(s, mod, norm_mix[l], _pack_w_in(w_in[l]), gmat, q_gain, k_gain,
                                     cos_tab, sin_tab, ssd_conv_w[l], ssd_conv_b[l],
                                     seq=seq, ctx_len=ctx_len, batch=batch)
        a_lat = _attention(q, k, v, attn_bias, l, seq=seq, ctx_len=ctx_len, batch=batch)
        a_ctx = _ctx_attention(q, k, v, seq=seq, ctx_len=ctx_len, batch=batch) if need_ctx else None
        s_o = _ssd(u, dt, z, ssd_dt_bias[l], ssd_a_log[l], ssd_d[l], ssd_norm[l],
                   seq=seq, ctx_len=ctx_len, batch=batch)
        n_out = n_all if need_ctx else n_lat
        s = _merge_ffn(s, n_out, mod, a_lat, a_ctx, s_o, g, na_w_o[l].astype(BF16), ssd_w_o[l].astype(BF16),
                       w_out[l].astype(BF16), norm_ffn2[l], ffn2_w_gate[l].astype(BF16),
                       ffn2_w_up[l].astype(BF16), ffn2_w_down[l].astype(BF16), seq=seq, batch=batch)
    return s[:n_lat].reshape(batch, seq, d)
```

```python
import functools

import numpy as np
import jax
import jax.numpy as jnp
from jax import lax
from jax.experimental import pallas as pl
from jax.experimental.pallas import tpu as pltpu

F32 = jnp.float32
BF16 = jnp.bfloat16

D_MODEL = 1024
GRID_W = 64
HEAD_DIM = 64
NA_HEADS = 16
NA_WIDTH = NA_HEADS * HEAD_DIM
WIN_R = 8
WIN_C = 16
ROPE_BASE = 10000.0
ATTN_SCALE = HEAD_DIM ** -0.5
SSD_D_INNER = 2 * D_MODEL
SSD_HEADDIM = 64
SSD_HEADS = SSD_D_INNER // SSD_HEADDIM
SSD_GROUPS = 4
SSD_STATE = 128
SSD_CONV = 5
SSD_CHUNK = 128
SSD_BC = SSD_GROUPS * SSD_STATE
SSD_CONV_DIM = SSD_D_INNER + 2 * SSD_BC
D_FF = 2816
N_MOD = 9
EPS = 1e-6

V7X_LANES = 128
V7X_SUBLANES = 8
V7X_MXU_DIM = 256
V7X_VMEM_LIMIT_BYTES = 56 * 1024 * 1024

HEAD_PAIR = V7X_LANES // HEAD_DIM
N_PAIRS = NA_HEADS // HEAD_PAIR
SSD_PAIRS = SSD_HEADS // HEAD_PAIR
PAIRS_PER_GROUP = SSD_PAIRS // SSD_GROUPS
DT_PAD = V7X_LANES
MASK_BIAS = -1e30
LOG2_E = 1.4426950408889634

TM_FFN = 1024
TM_PROJ = 512
TM_MERGE = 512
ATTN_ROWS_PER_STEP = 32
ATTN_GROUP_ROWS = 32
SSD_BLOCK = 2 * SSD_CHUNK
PROJ_HALO = 2 * V7X_SUBLANES
CONV_SEG = 256
FF_CHUNK = V7X_MXU_DIM
QK_CHUNK = V7X_MXU_DIM
PROJ_CHUNK = 512

OFF_Q = 0
OFF_K = OFF_Q + NA_WIDTH
OFF_V = OFF_K + NA_WIDTH
OFF_Z = OFF_V + NA_WIDTH
OFF_XBC = OFF_Z + SSD_D_INNER
OFF_G = OFF_XBC + SSD_CONV_DIM
OFF_DT = OFF_G + 2 * D_MODEL
N_IN_PACKED = OFF_DT + DT_PAD


def _cparams(n_axes):
    return pltpu.CompilerParams(dimension_semantics=("arbitrary",) * n_axes,
                                vmem_limit_bytes=V7X_VMEM_LIMIT_BYTES)


def _resident():
    return pl.BlockSpec(memory_space=pltpu.VMEM)


def _sigmoid(x):
    return 0.5 * jnp.tanh(0.5 * x) + 0.5


def _silu(x):
    h = 0.5 * x
    return h + h * jnp.tanh(h)


def _softplus(x):
    return jnp.maximum(x, 0.0) + jnp.log(1.0 + jnp.exp(-jnp.abs(x)))


def _dot(a, b):
    return jnp.dot(a, b, preferred_element_type=F32)


def _dot_nt(a, b):
    return lax.dot_general(a, b, (((1,), (1,)), ((), ())), preferred_element_type=F32)


def _norm_modulate(x, g_row, shift, scale):
    ms = jnp.mean(x * x, axis=-1, keepdims=True)
    y = x * lax.rsqrt(ms + EPS) * g_row
    return y * (1.0 + scale) + shift


def _adaln_kernel(v_ref, w_ref, b_ref, o_ref):
    sv = _silu(v_ref[...]).astype(BF16)
    o_ref[0] = _dot(sv, w_ref[0].astype(BF16)) + b_ref[0]


def _adaln(v_rows, w_ada, b_ada):
    depth, d, n = w_ada.shape
    rows = v_rows.shape[0]
    tn = D_MODEL
    return pl.pallas_call(
        _adaln_kernel,
        grid=(depth, n // tn),
        in_specs=[pl.BlockSpec((rows, d), lambda l, j: (0, 0)),
                  pl.BlockSpec((1, d, tn), lambda l, j: (l, 0, j)),
                  pl.BlockSpec((1, 1, tn), lambda l, j: (l, 0, j))],
        out_specs=pl.BlockSpec((1, rows, tn), lambda l, j: (l, 0, j)),
        out_shape=jax.ShapeDtypeStruct((depth, rows, n), F32),
        compiler_params=_cparams(2),
        name="adaln",
    )(v_rows, w_ada, b_ada.reshape(depth, 1, n))


def _ffn_body(x, mod_ref, g_ref, wg_ref, wu_ref, wd_ref, act_scr, mod_i):
    shift = mod_ref[0, mod_i:mod_i + 1, :]
    scale = mod_ref[0, mod_i + 1:mod_i + 2, :]
    gate = mod_ref[0, mod_i + 2:mod_i + 3, :]
    h = _norm_modulate(x, g_ref[...], shift, scale).astype(BF16)
    for j in range(D_FF // FF_CHUNK):
        cs = slice(j * FF_CHUNK, (j + 1) * FF_CHUNK)
        act_scr[:, cs] = (_silu(_dot(h, wg_ref[:, cs])) * _dot(h, wu_ref[:, cs])).astype(BF16)
    y = _dot(act_scr[...], wd_ref[...])
    return x + 0.5 * gate * y


def _ffn_kernel(s_ref, mod_ref, g_ref, wg_ref, wu_ref, wd_ref, o_ref, act_scr, *, mod_i):
    o_ref[...] = _ffn_body(s_ref[...], mod_ref, g_ref, wg_ref, wu_ref, wd_ref, act_scr, mod_i)


def _ffn(s, n_rows, mod, g_norm, wg, wu, wd, *, mod_i, seq, batch):
    tm = TM_FFN
    return pl.pallas_call(
        functools.partial(_ffn_kernel, mod_i=mod_i),
        grid=(n_rows // tm,),
        in_specs=[pl.BlockSpec((tm, D_MODEL), lambda i: (i, 0)),
                  pl.BlockSpec((1, N_MOD, D_MODEL), lambda i: (jnp.minimum(i * tm // seq, batch), 0, 0)),
                  pl.BlockSpec((1, D_MODEL), lambda i: (0, 0)),
                  _resident(), _resident(), _resident()],
        out_specs=pl.BlockSpec((tm, D_MODEL), lambda i: (i, 0)),
        out_shape=jax.ShapeDtypeStruct((n_rows, D_MODEL), F32),
        scratch_shapes=[pltpu.VMEM((tm, D_FF), BF16)],
        compiler_params=_cparams(1),
        name="ffn",
    )(s, mod, g_norm.reshape(1, D_MODEL), wg, wu, wd)


def _proj_kernel(s_ref, sp_ref, sn_ref, mod_ref, g_ref, w_ref, gmat_ref, qg_ref, kg_ref, cos_ref, sin_ref,
                 cw_ref, cbias_ref, q_ref, k_ref, v_ref, z_ref, u_ref, gg_ref, dt_ref, h_scr, slab_scr,
                 *, n_lat_tiles, seq, ctx_len):
    tm = s_ref.shape[0]

    def hidden(x):
        return _norm_modulate(x, g_ref[...], mod_ref[0, 3:4, :], mod_ref[0, 4:5, :]).astype(BF16)

    h_scr[0:PROJ_HALO, :] = hidden(sp_ref[...])
    h_scr[PROJ_HALO:PROJ_HALO + tm, :] = hidden(s_ref[...])
    h_scr[PROJ_HALO + tm:, :] = hidden(sn_ref[...])

    def h_main():
        return h_scr[PROJ_HALO:PROJ_HALO + tm, :]

    lane = lax.broadcasted_iota(jnp.int32, (tm, QK_CHUNK), 1)
    first_half = (lane % HEAD_DIM) < (HEAD_DIM // 2)
    cos = cos_ref[...]
    sin = sin_ref[...]

    def qk_raw(off, j):
        y = _dot(h_main(), w_ref[:, off + j * QK_CHUNK:off + (j + 1) * QK_CHUNK])
        return y, (y * y).astype(BF16)

    def qk_finish(gain_ref, out_ref, j, y, sq_split):
        cs = slice(j * QK_CHUNK, (j + 1) * QK_CHUNK)
        ms = _dot(sq_split, gmat_ref[...])
        yn = y * lax.rsqrt(ms + EPS) * gain_ref[:, cs]
        swapped = jnp.where(first_half,
                            pltpu.roll(yn, QK_CHUNK - HEAD_DIM // 2, 1),
                            pltpu.roll(yn, HEAD_DIM // 2, 1))
        out_ref[:, cs] = (yn * cos + swapped * sin).astype(out_ref.dtype)

    def plain_chunk(off, out_ref, j):
        y = _dot(h_main(), w_ref[:, off + j * PROJ_CHUNK:off + (j + 1) * PROJ_CHUNK])
        out_ref[:, j * PROJ_CHUNK:(j + 1) * PROJ_CHUNK] = y.astype(out_ref.dtype)

    tile = pl.program_id(0)
    is_ctx = tile >= n_lat_tiles
    row0 = jnp.where(is_ctx, tile - n_lat_tiles, tile) * tm
    seq_len = jnp.where(is_ctx, ctx_len, seq)
    seg_starts = [row0 + sg * CONV_SEG for sg in range(tm // CONV_SEG)]
    zero_prev = [lax.rem(st, seq_len) == 0 for st in seg_starts]
    zero_next = [lax.rem(st + CONV_SEG, seq_len) == 0 for st in seg_starts]

    def conv_chunk(j):
        y = _dot(h_scr[...], w_ref[:, OFF_XBC + j * PROJ_CHUNK:OFF_XBC + (j + 1) * PROJ_CHUNK])
        par = j % 2
        for lb in range(PROJ_CHUNK // V7X_LANES):
            cs = slice(j * PROJ_CHUNK + lb * V7X_LANES, j * PROJ_CHUNK + (lb + 1) * V7X_LANES)
            ylb = y[:, lb * V7X_LANES:(lb + 1) * V7X_LANES]
            for sg in range(tm // CONV_SEG):
                lo = sg * CONV_SEG
                slab_scr[par, lb, sg, 0:PROJ_HALO, :] = jnp.where(zero_prev[sg], 0.0, ylb[lo:lo + PROJ_HALO])
                slab_scr[par, lb, sg, PROJ_HALO:PROJ_HALO + CONV_SEG, :] = ylb[lo + PROJ_HALO:lo + PROJ_HALO + CONV_SEG]
                slab_scr[par, lb, sg, PROJ_HALO + CONV_SEG:, :] = jnp.where(
                    zero_next[sg], 0.0, ylb[lo + PROJ_HALO + CONV_SEG:lo + 2 * PROJ_HALO + CONV_SEG])
                acc = jnp.broadcast_to(cbias_ref[:, cs], (CONV_SEG, V7X_LANES))
                for t in range(SSD_CONV):
                    r0 = PROJ_HALO - SSD_CONV // 2 + t
                    acc = acc + slab_scr[par, lb, sg, r0:r0 + CONV_SEG, :] * cw_ref[t:t + 1, cs]
                u_ref[lo:lo + CONV_SEG, cs] = _silu(acc).astype(u_ref.dtype)

    qk_chunks = [(off, gain, out, j) for off, gain, out in ((OFF_Q, qg_ref, q_ref), (OFF_K, kg_ref, k_ref))
                 for j in range(NA_WIDTH // QK_CHUNK)]
    plain_tasks = [functools.partial(plain_chunk, off, out, j)
                   for off, width, out in ((OFF_V, NA_WIDTH, v_ref), (OFF_Z, SSD_D_INNER, z_ref),
                                           (OFF_G, 2 * D_MODEL, gg_ref))
                   for j in range(width // PROJ_CHUNK)]
    conv_tasks = [functools.partial(conv_chunk, j) for j in range(SSD_CONV_DIM // PROJ_CHUNK)]
    plain_tasks = [t for pair in zip(plain_tasks[:len(conv_tasks)], conv_tasks) for t in pair] \
        + plain_tasks[len(conv_tasks):]
    assert len(plain_tasks) == 2 * len(qk_chunks)
    for i, (off, gain, out, j) in enumerate(qk_chunks):
        y, sq_split = qk_raw(off, j)
        plain_tasks[2 * i]()
        qk_finish(gain, out, j, y, sq_split)
        plain_tasks[2 * i + 1]()
    dt_ref[...] = _dot(h_main(), w_ref[:, OFF_DT:OFF_DT + DT_PAD])


def _proj(s, mod, g_norm, w_packed, gmat, q_gain, k_gain, cos_tab, sin_tab, conv_w, conv_b, *, seq, ctx_len,
          batch):
    n_rows = s.shape[0]
    tm = TM_PROJ
    n_lat = batch * seq // tm
    tiles_per_seq = seq // tm
    halo_per_tile = tm // PROJ_HALO
    n_halo_blocks = n_rows // PROJ_HALO

    def rope_idx(i):
        return (jnp.where(i < n_lat, i % tiles_per_seq, tiles_per_seq), 0)

    def rows(width):
        return pl.BlockSpec((tm, width), lambda i: (i, 0))

    def const(n, width):
        return pl.BlockSpec((n, width), lambda i: (0, 0))

    def out(width, dtype):
        return jax.ShapeDtypeStruct((n_rows, width), dtype)

    conv_w_rows = jnp.zeros((V7X_SUBLANES, SSD_CONV_DIM), F32).at[:SSD_CONV].set(conv_w)
    slabs = pltpu.VMEM((2, PROJ_CHUNK // V7X_LANES, tm // CONV_SEG, CONV_SEG + 2 * PROJ_HALO, V7X_LANES), F32)
    return pl.pallas_call(
        functools.partial(_proj_kernel, n_lat_tiles=n_lat, seq=seq, ctx_len=ctx_len),
        grid=(n_rows // tm,),
        in_specs=[rows(D_MODEL),
                  pl.BlockSpec((PROJ_HALO, D_MODEL), lambda i: (jnp.maximum(i * halo_per_tile - 1, 0), 0)),
                  pl.BlockSpec((PROJ_HALO, D_MODEL),
                               lambda i: (jnp.minimum((i + 1) * halo_per_tile, n_halo_blocks - 1), 0)),
                  pl.BlockSpec((1, N_MOD, D_MODEL), lambda i: (jnp.minimum(i * tm // seq, batch), 0, 0)),
                  const(1, D_MODEL),
                  _resident(), _resident(),
                  const(1, NA_WIDTH), const(1, NA_WIDTH),
                  pl.BlockSpec((tm, QK_CHUNK), rope_idx),
                  pl.BlockSpec((tm, QK_CHUNK), rope_idx),
                  const(V7X_SUBLANES, SSD_CONV_DIM), const(1, SSD_CONV_DIM)],
        out_specs=[rows(NA_WIDTH), rows(NA_WIDTH), rows(NA_WIDTH), rows(SSD_D_INNER),
                   rows(SSD_CONV_DIM), rows(2 * D_MODEL), rows(DT_PAD)],
        out_shape=[out(NA_WIDTH, BF16), out(NA_WIDTH, BF16), out(NA_WIDTH, BF16), out(SSD_D_INNER, BF16),
                   out(SSD_CONV_DIM, BF16), out(2 * D_MODEL, BF16), out(DT_PAD, F32)],
        scratch_shapes=[pltpu.VMEM((tm + 2 * PROJ_HALO, D_MODEL), BF16), slabs],
        compiler_params=_cparams(1),
        name="in_proj",
    )(s, s, s, mod, g_norm.reshape(1, D_MODEL), w_packed, gmat, q_gain, k_gain, cos_tab, sin_tab,
      conv_w_rows, conv_b.reshape(1, SSD_CONV_DIM))


ATTN_STACK = HEAD_PAIR * GRID_W


def _stacked_queries(q_ref, n_grid_rows):
    low = lax.broadcasted_iota(jnp.int32, (GRID_W, V7X_LANES), 1) < HEAD_DIM
    parts = []
    for rr in range(n_grid_rows):
        qrow = q_ref[rr * GRID_W:(rr + 1) * GRID_W, :]
        zero = jnp.zeros_like(qrow)
        parts += [jnp.where(low, qrow, zero), jnp.where(low, zero, qrow)]
    return jnp.concatenate(parts, axis=0)


def _unstack_heads(o, denom, o_ref, n_grid_rows):
    low = lax.broadcasted_iota(jnp.int32, (GRID_W, V7X_LANES), 1) < HEAD_DIM
    o = o / denom
    rows = [jnp.where(low, o[rr * ATTN_STACK:rr * ATTN_STACK + GRID_W],
                      o[rr * ATTN_STACK + GRID_W:(rr + 1) * ATTN_STACK]) for rr in range(n_grid_rows)]
    o_ref[...] = jnp.concatenate(rows, axis=0).astype(o_ref.dtype)


def _attn_kernel(q_ref, k_ref, v_ref, kc_ref, vc_ref, bias_ref, o_ref, *, rows_per_step, grid_rows):
    rg = pl.program_id(2)
    win_keys = WIN_R * GRID_W
    qs = _stacked_queries(q_ref, rows_per_step)

    def window_start(rr):
        r = rg * rows_per_step + rr
        r0 = jnp.clip(r - WIN_R // 2, 0, grid_rows - WIN_R)
        return r, r0, pl.multiple_of(r0 * GRID_W, GRID_W)

    gs = ATTN_GROUP_ROWS

    def scores(g):
        sws = []
        for rr in range(g * gs, (g + 1) * gs):
            r, r0, start = window_start(rr)
            d0 = (WIN_R - 1) - (r - r0)
            bias = jnp.concatenate([bias_ref[0, d0 + i] for i in range(0, WIN_R, HEAD_PAIR)], axis=1)
            sws.append(_dot_nt(qs[rr * ATTN_STACK:(rr + 1) * ATTN_STACK], k_ref[pl.ds(start, win_keys), :]) + bias)
        sc = _dot_nt(qs[g * gs * ATTN_STACK:(g + 1) * gs * ATTN_STACK], kc_ref[...])
        return jnp.concatenate(sws, axis=0), sc

    def probs(sw, sc):
        m = jnp.maximum(jnp.max(sc, axis=-1, keepdims=True), jnp.max(sw, axis=-1, keepdims=True))
        return jnp.exp2((sw - m).astype(BF16)), jnp.exp2((sc - m).astype(BF16))

    def weighted(g, pw, pc):
        ones_w = jnp.ones((win_keys, V7X_LANES), BF16)
        ones_c = jnp.ones((kc_ref.shape[0], V7X_LANES), BF16)
        ows = []
        for i, rr in enumerate(range(g * gs, (g + 1) * gs)):
            _, _, start = window_start(rr)
            v_aug = jnp.concatenate([v_ref[pl.ds(start, win_keys), :], ones_w], axis=1)
            ows.append(_dot(pw[i * ATTN_STACK:(i + 1) * ATTN_STACK], v_aug))
        return jnp.concatenate(ows, axis=0) + _dot(pc, jnp.concatenate([vc_ref[...], ones_c], axis=1))

    outs = []
    for g in range(rows_per_step // gs):
        outs.append(weighted(g, *probs(*scores(g))))
    o_aug = jnp.concatenate(outs, axis=0)
    _unstack_heads(o_aug[:, :V7X_LANES], o_aug[:, V7X_LANES:], o_ref, rows_per_step)


def _attention(q, k, v, bias, layer, *, seq, ctx_len, batch):
    rows_per_step = ATTN_ROWS_PER_STEP
    tq = rows_per_step * GRID_W
    grid_rows = seq // GRID_W
    n_rg = grid_rows // rows_per_step
    ctx_blk0 = batch * seq // ctx_len
    return pl.pallas_call(
        functools.partial(_attn_kernel, rows_per_step=rows_per_step, grid_rows=grid_rows),
        grid=(N_PAIRS, batch, n_rg),
        in_specs=[pl.BlockSpec((tq, V7X_LANES), lambda j, b, rg: (b * n_rg + rg, j)),
                  pl.BlockSpec((seq, V7X_LANES), lambda j, b, rg: (b, j)),
                  pl.BlockSpec((seq, V7X_LANES), lambda j, b, rg: (b, j)),
                  pl.BlockSpec((ctx_len, V7X_LANES), lambda j, b, rg: (ctx_blk0 + b, j)),
                  pl.BlockSpec((ctx_len, V7X_LANES), lambda j, b, rg: (ctx_blk0 + b, j)),
                  pl.BlockSpec((1, 2 * WIN_R - 2, ATTN_STACK, 2 * GRID_W), lambda j, b, rg: (layer * N_PAIRS + j, 0, 0, 0))],
        out_specs=pl.BlockSpec((tq, V7X_LANES), lambda j, b, rg: (b * n_rg + rg, j)),
        out_shape=jax.ShapeDtypeStruct((batch * seq, NA_WIDTH), BF16),
        compiler_params=_cparams(3),
        name="attention",
    )(q, k, v, k, v, bias)


def _ctx_attn_kernel(q_ref, kc_ref, vc_ref, o_ref, *, n_grid_rows):
    qs = _stacked_queries(q_ref, n_grid_rows)
    sc = _dot_nt(qs, kc_ref[...])
    pc = jnp.exp2(sc - jnp.max(sc, axis=-1, keepdims=True))
    _unstack_heads(_dot(pc.astype(BF16), vc_ref[...]), jnp.sum(pc, axis=-1, keepdims=True), o_ref, n_grid_rows)


def _ctx_attention(q, k, v, *, seq, ctx_len, batch):
    ctx_blk0 = batch * seq // ctx_len
    spec = pl.BlockSpec((ctx_len, V7X_LANES), lambda j, b: (ctx_blk0 + b, j))
    return pl.pallas_call(
        functools.partial(_ctx_attn_kernel, n_grid_rows=ctx_len // GRID_W),
        grid=(N_PAIRS, batch),
        in_specs=[spec, spec, spec],
        out_specs=pl.BlockSpec((ctx_len, V7X_LANES), lambda j, b: (b, j)),
        out_shape=jax.ShapeDtypeStruct((batch * ctx_len, NA_WIDTH), BF16),
        compiler_params=_cparams(2),
        name="ctx_attention",
    )(q, k, v)


def _attention_bias(rpb_all):
    n_layers = rpb_all.shape[0]
    n_dr, n_dc = 2 * WIN_R - 1, 2 * WIN_C - 1
    c = np.arange(GRID_W)[:, None]
    kc = np.arange(GRID_W)[None, :]
    s_col = np.clip(c - WIN_C // 2, 0, GRID_W - WIN_C)
    valid = (kc >= s_col) & (kc < s_col + WIN_C)
    dc = kc - c + (WIN_C - 1)
    onehot = ((dc[None] == np.arange(n_dc)[:, None, None]) & valid[None]).astype(np.float32)
    sel = np.zeros((HEAD_PAIR * n_dc + 1, GRID_W, HEAD_PAIR * GRID_W), np.float32)
    for e in range(HEAD_PAIR):
        sel[e * n_dc:(e + 1) * n_dc, :, e * GRID_W:(e + 1) * GRID_W] = onehot
        sel[-1, :, e * GRID_W:(e + 1) * GRID_W] = np.where(valid, 0.0, MASK_BIAS)
    r = (rpb_all.astype(F32) * LOG2_E).reshape(n_layers, N_PAIRS, HEAD_PAIR, n_dr, n_dc)
    r = jnp.transpose(r, (0, 1, 3, 2, 4))
    feats = jnp.concatenate([r[:, :, :-1], r[:, :, 1:], jnp.ones(r.shape[:2] + (n_dr - 1, HEAD_PAIR, 1), F32)],
                            axis=-1)
    out = jnp.einsum('lpdhx,xcn->lpdhcn', feats, jnp.asarray(sel), precision=lax.Precision.HIGHEST)
    return out.reshape(n_layers * N_PAIRS, n_dr - 1, HEAD_PAIR * GRID_W, HEAD_PAIR * GRID_W)


def _scan_rows(x, reverse):
    n = x.shape[0]
    row = lax.broadcasted_iota(jnp.int32, x.shape, 0)
    step = 1
    while step < n:
        if reverse:
            x = x + jnp.where(row < n - step, pltpu.roll(x, n - step, 0), 0.0)
        else:
            x = x + jnp.where(row >= step, pltpu.roll(x, step, 0), 0.0)
        step *= 2
    return x


def _ssd_chunk(get_x, get_b, get_c, dt_raw, dtb_ref, alog_ref, st_scr, write_y, *, reverse):
    q = SSD_CHUNK
    direction = 1 if reverse else 0
    dt = _softplus(dt_raw + dtb_ref[...])
    a = -jnp.exp(alog_ref[...])
    ac = _scan_rows(dt * a, reverse)
    edge = 0 if reverse else q - 1
    e_all = jnp.exp(ac)
    wdt_t = (jnp.exp(ac[edge:edge + 1, :] - ac) * dt).T
    ac2 = ac * LOG2_E
    lg2_t = (ac2 - jnp.log2(dt)).T
    ii = lax.broadcasted_iota(jnp.int32, (q, q), 0)
    jj = lax.broadcasted_iota(jnp.int32, (q, q), 1)
    causal = (ii <= jj) if reverse else (ii >= jj)
    low = jj < SSD_HEADDIM
    for g in range(SSD_GROUPS):
        b_g = get_b(g)
        c_g = get_c(g)
        cb = _dot_nt(c_g.astype(BF16), b_g.astype(BF16))
        b_t = b_g.T
        for pp in range(PAIRS_PER_GROUP):
            pair = g * PAIRS_PER_GROUP + pp
            xp = get_x(pair)
            st = st_scr[pair]
            rhs = jnp.concatenate([xp, st.astype(BF16)], axis=0)
            zero = jnp.zeros_like(xp)
            x_split = jnp.concatenate([jnp.where(low, xp, zero), jnp.where(low, zero, xp)], axis=0)
            ys, bws, decays = [], [], []
            for half in range(HEAD_PAIR):
                hl = direction * SSD_HEADS + pair * HEAD_PAIR + half
                a_col = jnp.broadcast_to(ac2[:, hl:hl + 1], (q, q))
                e_col = jnp.broadcast_to(e_all[:, hl:hl + 1], (q, q))
                m = jnp.exp2(jnp.where(causal, a_col - lg2_t[hl:hl + 1, :], -jnp.inf)) * cb
                lhs = jnp.concatenate([m, c_g * e_col], axis=1).astype(BF16)
                ys.append(_dot(lhs, rhs))
                bws.append((b_t * wdt_t[hl:hl + 1, :]).astype(BF16))
                decays.append(e_col[edge:edge + 1, :])
            write_y(pair, jnp.where(low, ys[0], ys[1]))
            new = _dot(jnp.concatenate(bws, axis=1), x_split)
            st_scr[pair] = st * jnp.where(low[0:1], decays[0], decays[1]) + new


def _ssd_position(step, b, *, n_cb, n_lb, batch, reverse):
    is_ctx = step < n_cb
    loc = jnp.where(is_ctx, step, step - n_cb)
    if reverse:
        loc = jnp.where(is_ctx, n_cb - 1 - loc, n_lb - 1 - loc)
    blk = jnp.where(is_ctx, batch * n_lb + b * n_cb + loc, b * n_lb + loc)
    return blk, loc


def _ssd_chunk_from_u(u_ref, rs, dt_ref, dtb_ref, alog_ref, st_scr, write_y, *, reverse):
    def get_bc(g, off):
        return u_ref[rs, SSD_D_INNER + off + g * SSD_STATE:SSD_D_INNER + off + (g + 1) * SSD_STATE].astype(F32)

    _ssd_chunk(lambda pair: u_ref[rs, pair * V7X_LANES:(pair + 1) * V7X_LANES],
               lambda g: get_bc(g, 0), lambda g: get_bc(g, SSD_BC),
               dt_ref[rs, :], dtb_ref, alog_ref, st_scr, write_y, reverse=reverse)


def _ssd_fwd_kernel(u_ref, dt_ref, dtb_ref, alog_ref, y_ref, st_scr):
    step = pl.program_id(1)

    @pl.when(step == 0)
    def _():
        st_scr[...] = jnp.zeros_like(st_scr)

    for ci in range(SSD_BLOCK // SSD_CHUNK):
        rs = slice(ci * SSD_CHUNK, (ci + 1) * SSD_CHUNK)

        def write_y(pair, val, rs=rs):
            y_ref[rs, pair * V7X_LANES:(pair + 1) * V7X_LANES] = val.astype(y_ref.dtype)

        _ssd_chunk_from_u(u_ref, rs, dt_ref, dtb_ref, alog_ref, st_scr, write_y, reverse=False)


def _ssd_bwd_kernel(u_ref, dt_ref, z_ref, yf_ref, dtb_ref, alog_ref, dskip_ref, nw_ref,
                    so_ref, y_scr, st_scr):
    step = pl.program_id(1)

    @pl.when(step == 0)
    def _():
        st_scr[...] = jnp.zeros_like(st_scr)

    gw = SSD_D_INNER // SSD_GROUPS
    for ci in reversed(range(SSD_BLOCK // SSD_CHUNK)):
        rs = slice(ci * SSD_CHUNK, (ci + 1) * SSD_CHUNK)

        def write_y(pair, val):
            y_scr[:, pair * V7X_LANES:(pair + 1) * V7X_LANES] = val

        _ssd_chunk_from_u(u_ref, rs, dt_ref, dtb_ref, alog_ref, st_scr, write_y, reverse=True)
        for g in range(SSD_GROUPS):
            cs = slice(g * gw, (g + 1) * gw)
            y = yf_ref[rs, cs].astype(F32) + y_scr[:, cs] + dskip_ref[:, cs] * u_ref[rs, cs].astype(F32)
            y = y * _silu(z_ref[rs, cs].astype(F32))
            ms = jnp.mean(y * y, axis=-1, keepdims=True)
            so_ref[rs, cs] = (y * lax.rsqrt(ms + EPS) * nw_ref[:, cs]).astype(so_ref.dtype)


def _ssd(u, dt, z, dt_bias, a_log, d_skip, norm_w, *, seq, ctx_len, batch):
    n_rows = u.shape[0]
    rows = SSD_BLOCK
    n_cb, n_lb = ctx_len // rows, seq // rows
    steps = n_cb + n_lb

    def pos(reverse):
        return functools.partial(_ssd_position, n_cb=n_cb, n_lb=n_lb, batch=batch, reverse=reverse)

    def block_rows(width, reverse):
        return pl.BlockSpec((rows, width), lambda b, s: (pos(reverse)(s, b)[0], 0))

    def small(n, width):
        return pl.BlockSpec((n, width), lambda b, s: (0, 0))

    pad = DT_PAD - 2 * SSD_HEADS
    dtb_row = jnp.pad(dt_bias.reshape(1, 2 * SSD_HEADS), ((0, 0), (0, pad)))
    alog_row = jnp.pad(a_log.reshape(1, 2 * SSD_HEADS), ((0, 0), (0, pad)))
    state = pltpu.VMEM((SSD_PAIRS, SSD_STATE, V7X_LANES), F32)

    y_f = pl.pallas_call(
        _ssd_fwd_kernel,
        grid=(batch, steps),
        in_specs=[block_rows(SSD_CONV_DIM, False), block_rows(DT_PAD, False), small(1, DT_PAD), small(1, DT_PAD)],
        out_specs=block_rows(SSD_D_INNER, False),
        out_shape=jax.ShapeDtypeStruct((n_rows, SSD_D_INNER), BF16),
        scratch_shapes=[state],
        compiler_params=_cparams(2),
        name="ssd_fwd",
    )(u, dt, dtb_row, alog_row)

    d_row = jnp.repeat(d_skip, SSD_HEADDIM).reshape(1, SSD_D_INNER)
    return pl.pallas_call(
        _ssd_bwd_kernel,
        grid=(batch, steps),
        in_specs=[block_rows(SSD_CONV_DIM, True), block_rows(DT_PAD, True), block_rows(SSD_D_INNER, True),
                  block_rows(SSD_D_INNER, True), small(1, DT_PAD), small(1, DT_PAD),
                  small(1, SSD_D_INNER), small(1, SSD_D_INNER)],
        out_specs=block_rows(SSD_D_INNER, True),
        out_shape=jax.ShapeDtypeStruct((n_rows, SSD_D_INNER), BF16),
        scratch_shapes=[pltpu.VMEM((SSD_CHUNK, SSD_D_INNER), F32), state],
        compiler_params=_cparams(2),
        name="ssd_bwd",
    )(u, dt, z, y_f, dtb_row, alog_row, d_row, norm_w.reshape(1, SSD_D_INNER))


def _merge_ffn_kernel(*refs, n_lat_tiles, with_ctx):
    if with_ctx:
        (s_ref, mod_ref, a_ref, ac_ref, so_ref, g_ref, wa_ref, ws_ref, wo_ref,
         gn_ref, wg_ref, wu_ref, wd_ref, o_ref, act_scr) = refs
        a = jnp.where(pl.program_id(0) < n_lat_tiles, a_ref[...], ac_ref[...])
    else:
        (s_ref, mod_ref, a_ref, so_ref, g_ref, wa_ref, ws_ref, wo_ref,
         gn_ref, wg_ref, wu_ref, wd_ref, o_ref, act_scr) = refs
        a = a_ref[...]
    ya = _dot(a, wa_ref[...])
    ys = _dot(so_ref[...], ws_ref[...])
    g = _sigmoid(g_ref[...].astype(F32))
    merged = (g[:, :D_MODEL] * ya + g[:, D_MODEL:] * ys).astype(BF16)
    x = s_ref[...] + mod_ref[0, 5:6, :] * _dot(merged, wo_ref[...])
    o_ref[...] = _ffn_body(x, mod_ref, gn_ref, wg_ref, wu_ref, wd_ref, act_scr, 6)


def _merge_ffn(s, n_rows, mod, a_lat, a_ctx, s_o, g, wa, ws, wo, g_norm, wg, wu, wd, *, seq, batch):
    tm = TM_MERGE
    n_lat_tiles = batch * seq // tm
    with_ctx = a_ctx is not None

    def rows(width):
        return pl.BlockSpec((tm, width), lambda i: (i, 0))

    attn_specs = [pl.BlockSpec((tm, NA_WIDTH), lambda i: (jnp.minimum(i, n_lat_tiles - 1), 0))]
    attn_args = [a_lat]
    if with_ctx:
        attn_specs.append(pl.BlockSpec((tm, NA_WIDTH), lambda i: (jnp.maximum(i - n_lat_tiles, 0), 0)))
        attn_args.append(a_ctx)
    return pl.pallas_call(
        functools.partial(_merge_ffn_kernel, n_lat_tiles=n_lat_tiles, with_ctx=with_ctx),
        grid=(n_rows // tm,),
        in_specs=[rows(D_MODEL),
                  pl.BlockSpec((1, N_MOD, D_MODEL), lambda i: (jnp.minimum(i * tm // seq, batch), 0, 0)),
                  *attn_specs, rows(SSD_D_INNER), rows(2 * D_MODEL),
                  _resident(), _resident(), _resident(),
                  pl.BlockSpec((1, D_MODEL), lambda i: (0, 0)),
                  _resident(), _resident(), _resident()],
        out_specs=rows(D_MODEL),
        out_shape=jax.ShapeDtypeStruct((n_rows, D_MODEL), F32),
        scratch_shapes=[pltpu.VMEM((tm, D_FF), BF16)],
        compiler_params=_cparams(1),
        name="merge_ffn",
    )(s, mod, *attn_args, s_o, g, wa, ws, wo, g_norm.reshape(1, D_MODEL), wg, wu, wd)


def _rope_tables(seq, pad_rows):
    t = np.arange(seq)
    row = (t // GRID_W).astype(np.float32)
    col = (t % GRID_W).astype(np.float32)
    n_freq = HEAD_DIM // 4
    inv = np.float32(ROPE_BASE) ** (-np.arange(n_freq, dtype=np.float32) / np.float32(n_freq))
    ang = np.concatenate([row[:, None] * inv, col[:, None] * inv], axis=-1).astype(np.float32)
    cos, sin = np.cos(ang), np.sin(ang)
    reps = QK_CHUNK // HEAD_DIM
    cos_t = np.tile(np.concatenate([cos, cos], axis=-1), (1, reps))
    sin_t = np.tile(np.concatenate([-sin, sin], axis=-1), (1, reps))
    cos_t = np.concatenate([cos_t, np.ones((pad_rows, QK_CHUNK), np.float32)], axis=0)
    sin_t = np.concatenate([sin_t, np.zeros((pad_rows, QK_CHUNK), np.float32)], axis=0)
    return jnp.asarray(cos_t, F32), jnp.asarray(sin_t, F32)


def _pack_w_in(w_in_l):
    sizes = (NA_WIDTH, NA_WIDTH, NA_WIDTH, SSD_D_INNER, SSD_CONV_DIM, 2 * SSD_HEADS, 2 * D_MODEL)
    offs = np.concatenate([[0], np.cumsum(sizes)])
    q, k, v, z, xbc, dt, g = (w_in_l[:, offs[i]:offs[i + 1]] for i in range(7))
    pad = jnp.zeros((D_MODEL, DT_PAD - 2 * SSD_HEADS), w_in_l.dtype)
    return jnp.concatenate([q, k, v, z, xbc, g, dt, pad], axis=1).astype(BF16)


def kernel(x, c, ctx, c_ctx, w_ada, b_ada, norm_ffn1, ffn1_w_gate, ffn1_w_up, ffn1_w_down, norm_mix, w_in,
           q_norm, k_norm, na_rpb, na_w_o, ssd_conv_w, ssd_conv_b, ssd_dt_bias, ssd_a_log, ssd_d, ssd_norm,
           ssd_w_o, w_out, norm_ffn2, ffn2_w_gate, ffn2_w_up, ffn2_w_down):
    batch, seq, d = x.shape
    ctx_len = ctx.shape[1]
    depth = w_ada.shape[0]
    n_lat = batch * seq
    n_all = n_lat + batch * ctx_len
    assert d == D_MODEL and seq % GRID_W == 0 and seq // GRID_W >= WIN_R
    assert seq % TM_FFN == 0 and (batch * ctx_len) % TM_FFN == 0 and seq % TM_PROJ == 0
    assert ctx_len % SSD_BLOCK == 0 and seq % ctx_len == 0 and ctx_len % GRID_W == 0
    assert (seq // GRID_W) % ATTN_ROWS_PER_STEP == 0 and (batch * ctx_len) % TM_MERGE == 0

    s = jnp.concatenate([x.reshape(n_lat, d), ctx.reshape(batch * ctx_len, d)], axis=0)

    mod_rows = -(-(batch + 1) // V7X_SUBLANES) * V7X_SUBLANES
    v_rows = jnp.zeros((mod_rows, d), F32).at[:batch].set(c).at[batch].set(c_ctx)
    mod_all = _adaln(v_rows, w_ada, b_ada).reshape(depth, mod_rows, N_MOD, d)

    cos_tab, sin_tab = _rope_tables(seq, TM_PROJ)
    gmat = jnp.asarray(np.kron(np.eye(QK_CHUNK // HEAD_DIM), np.full((HEAD_DIM, HEAD_DIM), 1.0 / HEAD_DIM)), BF16)

    attn_bias = _attention_bias(na_rpb)

    for l in range(depth):
        need_ctx = l < depth - 1
        mod = mod_all[l]
        s = _ffn(s, n_all, mod, norm_ffn1[l], ffn1_w_gate[l].astype(BF16), ffn1_w_up[l].astype(BF16),
                 ffn1_w_down[l].astype(BF16), mod_i=0, seq=seq, batch=batch)
        q_gain = jnp.tile(q_norm[l] * (ATTN_SCALE * LOG2_E), NA_HEADS).reshape(1, NA_WIDTH)
        k_gain = jnp.tile(k_norm[l], NA_HEADS).reshape(1, NA_WIDTH)
        q, k, v, z, u, g, dt = _proj(s, mod, norm_mix[l], _pack_w_in(w_in[l]), gmat, q_gain, k_gain,
                                     cos_tab, sin_tab, ssd_conv_w[l], ssd_conv_b[l],
                                     seq=seq, ctx_len=ctx_len, batch=batch)
        a_lat = _attention(q, k, v, attn_bias, l, seq=seq, ctx_len=ctx_len, batch=batch)
        a_ctx = _ctx_attention(q, k, v, seq=seq, ctx_len=ctx_len, batch=batch) if need_ctx else None
        s_o = _ssd(u, dt, z, ssd_dt_bias[l], ssd_a_log[l], ssd_d[l], ssd_norm[l],
                   seq=seq, ctx_len=ctx_len, batch=batch)
        n_out = n_all if need_ctx else n_lat
        s = _merge_ffn(s, n_out, mod, a_lat, a_ctx, s_o, g, na_w_o[l].astype(BF16), ssd_w_o[l].astype(BF16),
                       w_out[l].astype(BF16), norm_ffn2[l], ffn2_w_gate[l].astype(BF16),
                       ffn2_w_up[l].astype(BF16), ffn2_w_down[l].astype(BF16), seq=seq, batch=batch)
    return s[:n_lat].reshape(batch, seq, d)
```
